```python
import math
import jax, jax.numpy as jnp
from jax import lax
import numpy as np

D_MODEL = 2048
BATCH = 8
SEQ = 2048
DEPTH = 1

MEM_LEN = 256
RMS_EPS = 1e-6
CONV_WIDTH = D_MODEL // 2
CONV_K = 3
N_MLA_HEADS = 8
QK_NOPE = 128
QK_ROPE = 64
V_HEAD = 128
Q_LORA = 512
KV_LORA = 256
ATTN_WIDTH = N_MLA_HEADS * V_HEAD
ROPE_THETA = 10000.0
Q_BLOCK = 128
MASK_VALUE = -1e30
IN_COLS = 3 * CONV_WIDTH + Q_LORA + KV_LORA + QK_ROPE
N_X_HEADS = 4
X_HEAD_DIM = D_MODEL // N_X_HEADS
N_GROUPS = 8
EXPERTS_PER_GROUP = 8
N_EXPERTS = N_GROUPS * EXPERTS_PER_GROUP
TOP_K = 2
D_EXPERT = 512
MOE_BLOCK = 256

kernel_name = "hymba_conv_mla_hier_moe"


def rmsnorm(x, g):
    xf = x.astype(jnp.float32)
    y = xf * lax.rsqrt(jnp.mean(xf * xf, axis=-1, keepdims=True) + RMS_EPS) * g.astype(jnp.float32)
    return y.astype(x.dtype)


def rope_tables(positions, dtype):
    inv_freq = 1.0 / (ROPE_THETA ** (jnp.arange(0, QK_ROPE, 2, dtype=jnp.float32) / QK_ROPE))
    ang = positions.astype(jnp.float32)[..., None] * inv_freq
    return jnp.cos(ang).astype(dtype), jnp.sin(ang).astype(dtype)


def apply_rope(x, cos, sin):
    x1, x2 = jnp.split(x, 2, axis=-1)
    return jnp.concatenate([x1 * cos - x2 * sin, x2 * cos + x1 * sin], axis=-1)


def short_conv_branch(b_gate, c_gate, h, w_conv):
    u = c_gate * h
    conv = lax.conv_general_dilated(
        u, w_conv[:, None, :].astype(u.dtype), window_strides=(1,),
        padding=[(CONV_K - 1, 0)], dimension_numbers=("NWC", "WIO", "NWC"),
        feature_group_count=CONV_WIDTH)
    return b_gate * conv


def causal_mla_attention(q_nope, q_rope, k_nope, k_rope, v):
    B, H, T, _ = q_nope.shape
    n_blk = T // Q_BLOCK
    scale = 1.0 / math.sqrt(QK_NOPE + QK_ROPE)
    kpos = jnp.arange(T)

    def one_block(i):
        s0 = i * Q_BLOCK
        qn = lax.dynamic_slice_in_dim(q_nope, s0, Q_BLOCK, axis=2)
        qr = lax.dynamic_slice_in_dim(q_rope, s0, Q_BLOCK, axis=2)
        s = (jnp.einsum("bhqd,bhkd->bhqk", qn, k_nope)
             + jnp.einsum("bhqr,bkr->bhqk", qr, k_rope)).astype(jnp.float32) * scale
        qpos = s0 + jnp.arange(Q_BLOCK)
        s = jnp.where(kpos[None, :] <= qpos[:, None], s, MASK_VALUE)
        p = jax.nn.softmax(s, axis=-1).astype(v.dtype)
        return jnp.einsum("bhqk,bhkd->bhqd", p, v)

    out = lax.map(one_block, jnp.arange(n_blk))
    return out.transpose(1, 0, 3, 2, 4).reshape(B, T, H * V_HEAD)


def mla_branch(c_q, c_kv, k_rope_raw, positions, g_q_a, w_q_up, g_kv_a, w_kv_up):
    B, T, _ = c_q.shape
    q = (rmsnorm(c_q, g_q_a) @ w_q_up).reshape(B, T, N_MLA_HEADS, QK_NOPE + QK_ROPE)
    q_nope, q_rope = q[..., :QK_NOPE], q[..., QK_NOPE:]
    kv = (rmsnorm(c_kv, g_kv_a) @ w_kv_up).reshape(B, T, N_MLA_HEADS, QK_NOPE + V_HEAD)
    k_nope, v = kv[..., :QK_NOPE], kv[..., QK_NOPE:]
    cos, sin = rope_tables(positions, c_q.dtype)
    q_rope = apply_rope(q_rope, cos[:, :, None, :], sin[:, :, None, :])
    k_rope = apply_rope(k_rope_raw, cos, sin)
    tr = lambda a: a.transpose(0, 2, 1, 3)
    return causal_mla_attention(tr(q_nope), tr(q_rope), tr(k_nope), k_rope, tr(v))


def memory_cross_attention(h, m, w_xq, w_xk, w_xv, w_xo):
    B, T, _ = h.shape
    q = (h @ w_xq).reshape(B, T, N_X_HEADS, X_HEAD_DIM)
    k = (m @ w_xk).reshape(B, m.shape[1], N_X_HEADS, X_HEAD_DIM)
    v = (m @ w_xv).reshape(B, m.shape[1], N_X_HEADS, X_HEAD_DIM)
    s = jnp.einsum("bqhd,bkhd->bhqk", q, k).astype(jnp.float32) * (1.0 / math.sqrt(X_HEAD_DIM))
    p = jax.nn.softmax(s, axis=-1).astype(v.dtype)
    o = jnp.einsum("bhqk,bkhd->bqhd", p, v).reshape(B, T, D_MODEL)
    return o @ w_xo


def hierarchical_moe(h, w_group, b_group, w_router, b_router, w_gate, w_up, w_down):
    B, T, D = h.shape
    n_tok = B * T
    xt = h.reshape(n_tok, D)
    xf = xt.astype(jnp.float32)
    g_logits = xf @ w_group.astype(jnp.float32) + b_group.astype(jnp.float32)
    g_prob = jax.nn.softmax(g_logits, axis=-1)
    g_p, g_idx = lax.top_k(g_prob, 1)
    e_logits = (xf @ w_router.astype(jnp.float32) + b_router.astype(jnp.float32)
                ).reshape(n_tok, N_GROUPS, EXPERTS_PER_GROUP)
    e_logits = jnp.take_along_axis(e_logits, g_idx[:, :, None], axis=1)[:, 0]
    e_prob = jax.nn.softmax(e_logits, axis=-1)
    t_p, t_i = lax.top_k(e_prob, TOP_K)
    t_p = t_p / jnp.sum(t_p, axis=-1, keepdims=True)
    expert_id = g_idx * EXPERTS_PER_GROUP + t_i
    gate_w = g_p * t_p

    n_slot = n_tok * TOP_K
    flat_e = expert_id.reshape(-1)
    flat_tok = jnp.repeat(jnp.arange(n_tok, dtype=jnp.int32), TOP_K)
    flat_w = gate_w.reshape(-1)
    order = jnp.argsort(flat_e, stable=True)
    s_e, s_tok, s_w = flat_e[order], flat_tok[order], flat_w[order]
    counts = jnp.bincount(flat_e, length=N_EXPERTS)
    padded = ((counts + MOE_BLOCK - 1) // MOE_BLOCK) * MOE_BLOCK
    pad_end = jnp.cumsum(padded)
    pad_start = pad_end - padded
    start = jnp.cumsum(counts) - counts
    dest = pad_start[s_e] + (jnp.arange(n_slot) - start[s_e])
    n_blk = (n_slot + MOE_BLOCK - 1) // MOE_BLOCK + N_EXPERTS
    L = n_blk * MOE_BLOCK
    buf_tok = jnp.full((L,), n_tok, dtype=jnp.int32).at[dest].set(s_tok)
    buf_w = jnp.zeros((L,), jnp.float32).at[dest].set(s_w)
    blk_expert = jnp.minimum(
        jnp.searchsorted(pad_end, jnp.arange(n_blk) * MOE_BLOCK, side="right"), N_EXPERTS - 1)
    x_pad = jnp.concatenate([xt, jnp.zeros((1, D), xt.dtype)], axis=0)

    def expert_block(args):
        tok, e = args
        xb = x_pad[tok]
        a = jax.nn.silu(xb @ w_gate[e]) * (xb @ w_up[e])
        return a @ w_down[e]

    y = lax.map(expert_block, (buf_tok.reshape(n_blk, MOE_BLOCK), blk_expert)).reshape(L, D)
    y = y * buf_w[:, None].astype(y.dtype)
    out = jnp.zeros((n_tok, D), xt.dtype).at[buf_tok].add(y, mode="drop")
    return out.reshape(B, T, D)


def setup_inputs(seed: int = 0) -> dict:
    key = jax.random.key(seed)
    ks = iter(jax.random.split(key, 32))
    f32 = jnp.float32
    nrm = lambda shape, fan_in: jax.random.normal(next(ks), shape, f32) * (fan_in ** -0.5)
    gain = lambda shape: 1.0 + 0.02 * jax.random.normal(next(ks), shape, f32)
    bias = lambda shape: 0.01 * jax.random.normal(next(ks), shape, f32)
    return {
        "x": jax.random.normal(next(ks), (BATCH, SEQ, D_MODEL), f32),
        "mem": jax.random.normal(next(ks), (BATCH, MEM_LEN, D_MODEL), f32),
        "positions": jnp.broadcast_to(jnp.arange(SEQ, dtype=jnp.int32), (BATCH, SEQ)),
        "g_mix": gain((DEPTH, D_MODEL)),
        "w_in": nrm((DEPTH, D_MODEL, IN_COLS), D_MODEL),
        "w_conv": nrm((DEPTH, CONV_K, CONV_WIDTH), CONV_K),
        "g_q_a": gain((DEPTH, Q_LORA)),
        "w_q_up": nrm((DEPTH, Q_LORA, N_MLA_HEADS * (QK_NOPE + QK_ROPE)), Q_LORA),
        "g_kv_a": gain((DEPTH, KV_LORA)),
        "w_kv_up": nrm((DEPTH, KV_LORA, N_MLA_HEADS * (QK_NOPE + V_HEAD)), KV_LORA),
        "g_conv_out": gain((DEPTH, CONV_WIDTH)),
        "g_attn_out": gain((DEPTH, ATTN_WIDTH)),
        "w_out": nrm((DEPTH, CONV_WIDTH + ATTN_WIDTH, D_MODEL), CONV_WIDTH + ATTN_WIDTH),
        "g_xattn": gain((DEPTH, D_MODEL)),
        "g_mem": gain((DEPTH, D_MODEL)),
        "w_xq": nrm((DEPTH, D_MODEL, D_MODEL), D_MODEL),
        "w_xk": nrm((DEPTH, D_MODEL, D_MODEL), D_MODEL),
        "w_xv": nrm((DEPTH, D_MODEL, D_MODEL), D_MODEL),
        "w_xo": nrm((DEPTH, D_MODEL, D_MODEL), D_MODEL),
        "g_moe": gain((DEPTH, D_MODEL)),
        "w_group": nrm((DEPTH, D_MODEL, N_GROUPS), D_MODEL),
        "b_group": bias((DEPTH, N_GROUPS)),
        "w_router": nrm((DEPTH, D_MODEL, N_EXPERTS), D_MODEL),
        "b_router": bias((DEPTH, N_EXPERTS)),
        "w_gate": nrm((DEPTH, N_EXPERTS, D_MODEL, D_EXPERT), D_MODEL),
        "w_up": nrm((DEPTH, N_EXPERTS, D_MODEL, D_EXPERT), D_MODEL),
        "w_down": nrm((DEPTH, N_EXPERTS, D_EXPERT, D_MODEL), D_EXPERT),
        "g_final": gain((D_MODEL,)),
    }


def reference(x, mem, positions, g_mix, w_in, w_conv, g_q_a, w_q_up, g_kv_a, w_kv_up,
              g_conv_out, g_attn_out, w_out, g_xattn, g_mem, w_xq, w_xk, w_xv, w_xo,
              g_moe, w_group, b_group, w_router, b_router, w_gate, w_up, w_down, g_final):
    c0 = 0
    c1 = c0 + CONV_WIDTH
    c2 = c1 + CONV_WIDTH
    c3 = c2 + CONV_WIDTH
    c4 = c3 + Q_LORA
    c5 = c4 + KV_LORA
    c6 = c5 + QK_ROPE
    for l in range(DEPTH):
        h = rmsnorm(x, g_mix[l])
        z = h @ w_in[l]
        y_conv = short_conv_branch(z[..., c0:c1], z[..., c1:c2], z[..., c2:c3], w_conv[l])
        y_attn = mla_branch(z[..., c3:c4], z[..., c4:c5], z[..., c5:c6], positions,
                            g_q_a[l], w_q_up[l], g_kv_a[l], w_kv_up[l])
        y = jnp.concatenate([rmsnorm(y_conv, g_conv_out[l]), rmsnorm(y_attn, g_attn_out[l])], axis=-1)
        x = x + y @ w_out[l]
        x = x + memory_cross_attention(rmsnorm(x, g_xattn[l]), rmsnorm(mem, g_mem[l]),
                                       w_xq[l], w_xk[l], w_xv[l], w_xo[l])
        x = x + hierarchical_moe(rmsnorm(x, g_moe[l]), w_group[l], b_group[l], w_router[l],
                                 b_router[l], w_gate[l], w_up[l], w_down[l])
    return rmsnorm(x, g_final)
```

```python
import functools
import math

import jax
import jax.numpy as jnp
from jax import lax
from jax.experimental import pallas as pl
from jax.experimental.pallas import tpu as pltpu

F32 = jnp.float32
BF16 = jnp.bfloat16

RMS_EPS = 1e-6
N_MLA_HEADS = 8
QK_NOPE = 128
QK_ROPE = 64
V_HEAD = 128
ROPE_THETA = 10000.0
MASK_VALUE = -1e30
N_X_HEADS = 4
N_GROUPS = 8
EXPERTS_PER_GROUP = 8
N_EXPERTS = N_GROUPS * EXPERTS_PER_GROUP
TOP_K = 2
MOE_BLOCK = 256

LANES = 128
VMEM_LIMIT_BYTES = 60000 * 1024

ROW_TILE = 256
ATTN_TILE = 512


def _rms(x, g):
    ms = jnp.mean(x * x, axis=-1, keepdims=True)
    return x * lax.rsqrt(ms + RMS_EPS) * g


def _resident(shape):
    nd = len(shape)
    return pl.BlockSpec(shape, lambda *_: (0,) * nd, pipeline_mode=pl.Buffered(1))


def _params(n_axes):
    return pltpu.CompilerParams(dimension_semantics=("arbitrary",) * n_axes,
                                vmem_limit_bytes=VMEM_LIMIT_BYTES)


def _in_proj_kernel(x_ref, pos_ref, gmix_ref, w_ref, wconv_ref, gq_ref, gkv_ref, gco_ref, freq_ref,
                    ycn_ref, cqn_ref, ckvn_ref, kr_ref, cs_ref, carry_ref,
                    *, tiles_per_seq, cw, ql, kvl):
    i = pl.program_id(0)
    tm = x_ref.shape[0]
    xn = _rms(x_ref[...], gmix_ref[...]).astype(BF16)

    def mm(lo, hi):
        return jnp.dot(xn, w_ref[:, lo:hi], preferred_element_type=F32)

    zb, zc, zh = mm(0, cw), mm(cw, 2 * cw), mm(2 * cw, 3 * cw)
    u = zc * zh

    @pl.when(i % tiles_per_seq == 0)
    def _():
        carry_ref[...] = jnp.zeros_like(carry_ref)

    prev = carry_ref[...]
    carry_ref[...] = u[tm - 8:, :]
    row = lax.broadcasted_iota(jnp.int32, (tm, 1), 0)
    u1 = jnp.where(row == 0, prev[7:8, :], pltpu.roll(u, 1, axis=0))
    u2 = jnp.where(row == 0, prev[6:7, :],
                   jnp.where(row == 1, prev[7:8, :], pltpu.roll(u, 2, axis=0)))
    wc = wconv_ref[...]
    conv = wc[0:1, :] * u2 + wc[1:2, :] * u1 + wc[2:3, :] * u
    ycn_ref[...] = _rms(zb * conv, gco_ref[...]).astype(BF16)

    c3 = 3 * cw
    cqn_ref[...] = _rms(mm(c3, c3 + ql), gq_ref[...]).astype(BF16)
    ckvn_ref[...] = _rms(mm(c3 + ql, c3 + ql + kvl), gkv_ref[...]).astype(BF16)

    zk = mm(c3 + ql + kvl, c3 + ql + kvl + LANES)
    ang = pos_ref[...].astype(F32) * freq_ref[...]
    cos, sin = jnp.cos(ang), jnp.sin(ang)
    lane = lax.broadcasted_iota(jnp.int32, (tm, LANES), 1)
    lo = lane < QK_ROPE
    cos_sin = jnp.where(lo, cos, sin)
    t = zk * cos_sin
    kr = t + pltpu.roll(t, QK_ROPE, axis=1)
    kr_ref[...] = jnp.where(lo, kr, 0.0).astype(BF16)
    cs_ref[...] = jnp.concatenate([cos_sin, jnp.where(lo, sin, cos)], axis=-1)


def _in_proj(x2d, pos2d, g_mix, w_in_ext, w_conv, g_q_a, g_kv_a, g_conv_out, freq, *, seq, cw, ql, kvl):
    n, d = x2d.shape
    tm = ROW_TILE
    wcols = w_in_ext.shape[1]
    kern = functools.partial(_in_proj_kernel, tiles_per_seq=seq // tm, cw=cw, ql=ql, kvl=kvl)
    row = lambda c: pl.BlockSpec((tm, c), lambda i: (i, 0))
    return pl.pallas_call(
        kern,
        grid=(n // tm,),
        in_specs=[row(d), row(1), _resident((1, d)), _resident((d, wcols)), _resident((3, cw)),
                  _resident((1, ql)), _resident((1, kvl)), _resident((1, cw)), _resident((1, LANES))],
        out_specs=[row(cw), row(ql), row(kvl), row(LANES), row(2 * LANES)],
        out_shape=[jax.ShapeDtypeStruct((n, cw), BF16), jax.ShapeDtypeStruct((n, ql), BF16),
                   jax.ShapeDtypeStruct((n, kvl), BF16), jax.ShapeDtypeStruct((n, LANES), BF16),
                   jax.ShapeDtypeStruct((n, 2 * LANES), F32)],
        scratch_shapes=[pltpu.VMEM((8, cw), F32)],
        compiler_params=_params(1),
        name="in_proj",
    )(x2d, pos2d, g_mix, w_in_ext, w_conv, g_q_a, g_kv_a, g_conv_out, freq)


def _attn_kernel(cqn_ref, cs_ref, ckvn_ref, kr_ref, wqa_ref, wqb_ref, wkv_ref, gao_ref,
                 out_ref, k_scr, v_scr, o_scr, m_scr, l_scr, acc_scr, *, scale):
    qi = pl.program_id(1)
    h = pl.program_id(2)
    n_heads = pl.num_programs(2)
    tq = cqn_ref.shape[0]
    tk = tq

    @pl.when(qi == 0)
    def _():
        kv = jnp.dot(ckvn_ref[...], wkv_ref[...], preferred_element_type=F32)
        k_scr[h, :, 0:QK_NOPE] = kv[:, :QK_NOPE].astype(BF16)
        k_scr[h, :, QK_NOPE:2 * QK_NOPE] = kr_ref[...]
        v_scr[h] = kv[:, QK_NOPE:].astype(BF16)

    cq = cqn_ref[...]
    cs = cs_ref[...]
    ones = jnp.ones((tq, LANES), F32)
    m1 = jnp.concatenate([ones, cs[:, :LANES]], axis=-1) * scale
    m2 = jnp.concatenate([ones, cs[:, LANES:]], axis=-1) * scale
    q = (jnp.dot(cq, wqa_ref[...], preferred_element_type=F32) * m1
         + jnp.dot(cq, wqb_ref[...], preferred_element_type=F32) * m2).astype(BF16)

    m_scr[...] = jnp.full_like(m_scr, -jnp.inf)
    l_scr[...] = jnp.zeros_like(l_scr)
    acc_scr[...] = jnp.zeros_like(acc_scr)

    def step(kb, masked):
        start = pl.multiple_of(kb * tk, tk)
        k = k_scr[h, pl.ds(start, tk), :]
        s = lax.dot_general(q, k, (((1,), (1,)), ((), ())), preferred_element_type=F32)
        if masked:
            r = lax.broadcasted_iota(jnp.int32, (tq, tk), 0)
            c = lax.broadcasted_iota(jnp.int32, (tq, tk), 1)
            s = jnp.where(c <= r, s, MASK_VALUE)
        m_prev = m_scr[...]
        m_new = jnp.maximum(m_prev, jnp.max(s, axis=-1, keepdims=True))
        alpha = jnp.exp(m_prev - m_new)
        p = jnp.exp(s - m_new)
        l_scr[...] = alpha * l_scr[...] + jnp.sum(p, axis=-1, keepdims=True)
        pv = jnp.dot(p.astype(BF16), v_scr[h, pl.ds(start, tk), :], preferred_element_type=F32)
        acc_scr[...] = alpha * acc_scr[...] + pv
        m_scr[...] = m_new

    def body(kb, carry):
        step(kb, False)
        return carry

    lax.fori_loop(0, qi, body, 0)
    step(qi, True)
    o_scr[h] = acc_scr[...] / l_scr[...]

    @pl.when(h == n_heads - 1)
    def _():
        o = jnp.concatenate([o_scr[j] for j in range(N_MLA_HEADS)], axis=-1)
        out_ref[...] = _rms(o, gao_ref[...]).astype(BF16)


def _attention(cqn, cs, ckvn, kr, wqa, wqb, wkv, g_attn_out, *, batch, seq):
    n, ql = cqn.shape
    kvl = ckvn.shape[1]
    tq = ATTN_TILE
    nq = seq // tq
    hd = QK_NOPE + V_HEAD
    aw = N_MLA_HEADS * V_HEAD
    kern = functools.partial(_attn_kernel, scale=1.0 / math.sqrt(QK_NOPE + QK_ROPE))
    qrow = lambda c: pl.BlockSpec((tq, c), lambda b, qi, h: (b * nq + qi, 0))
    brow = lambda c: pl.BlockSpec((seq, c), lambda b, qi, h: (b, 0))
    head = lambda r, c: pl.BlockSpec((None, r, c), lambda b, qi, h: (h, 0, 0))
    return pl.pallas_call(
        kern,
        grid=(batch, nq, N_MLA_HEADS),
        in_specs=[qrow(ql), qrow(2 * LANES), brow(kvl), brow(LANES),
                  head(ql, 2 * QK_NOPE), head(ql, 2 * QK_NOPE), head(kvl, hd),
                  pl.BlockSpec((1, aw), lambda b, qi, h: (0, 0))],
        out_specs=qrow(aw),
        out_shape=jax.ShapeDtypeStruct((n, aw), BF16),
        scratch_shapes=[pltpu.VMEM((N_MLA_HEADS, seq, 2 * QK_NOPE), BF16),
                        pltpu.VMEM((N_MLA_HEADS, seq, V_HEAD), BF16),
                        pltpu.VMEM((N_MLA_HEADS, tq, V_HEAD), F32),
                        pltpu.VMEM((tq, 1), F32), pltpu.VMEM((tq, 1), F32),
                        pltpu.VMEM((tq, V_HEAD), F32)],
        compiler_params=_params(3),
        name="mla_attention",
    )(cqn, cs, ckvn, kr, wqa, wqb, wkv, g_attn_out)


def _mem_kv_kernel(mem_ref, g_ref, w_ref, k_ref, v_ref):
    d = mem_ref.shape[1]
    mn = _rms(mem_ref[...], g_ref[...]).astype(BF16)
    k_ref[...] = jnp.dot(mn, w_ref[:, :d], preferred_element_type=F32).astype(BF16)
    v_ref[...] = jnp.dot(mn, w_ref[:, d:], preferred_element_type=F32).astype(BF16)


def _mem_kv(mem2d, g_mem, w_xkv, *, mem_len):
    n, d = mem2d.shape
    row = pl.BlockSpec((mem_len, d), lambda i: (i, 0))
    return pl.pallas_call(
        _mem_kv_kernel,
        grid=(n // mem_len,),
        in_specs=[row, _resident((1, d)), _resident((d, 2 * d))],
        out_specs=[row, row],
        out_shape=[jax.ShapeDtypeStruct((n, d), BF16)] * 2,
        compiler_params=_params(1),
        name="mem_kv",
    )(mem2d, g_mem, w_xkv)


def _route(logits):
    tm = logits.shape[0]
    lane = lax.broadcasted_iota(jnp.int32, (tm, LANES), 1)
    big = jnp.int32(1 << 20)
    ninf = -jnp.inf
    gl = jnp.where(lane < N_GROUPS, logits, ninf)
    gm = jnp.max(gl, axis=-1, keepdims=True)
    gidx = jnp.min(jnp.where(gl == gm, lane, big), axis=-1, keepdims=True)
    g_p = 1.0 / jnp.sum(jnp.exp(gl - gm), axis=-1, keepdims=True)
    first = N_GROUPS + gidx * EXPERTS_PER_GROUP
    el = jnp.where(lane >= first, jnp.where(lane < first + EXPERTS_PER_GROUP, logits, ninf), ninf)
    top1 = jnp.max(el, axis=-1, keepdims=True)
    i1 = jnp.min(jnp.where(el == top1, lane, big), axis=-1, keepdims=True)
    el2 = jnp.where(lane == i1, ninf, el)
    top2 = jnp.max(el2, axis=-1, keepdims=True)
    i2 = jnp.min(jnp.where(el2 == top2, lane, big), axis=-1, keepdims=True)
    r = jnp.exp(top2 - top1)
    w1 = g_p / (1.0 + r)
    w2 = g_p * r / (1.0 + r)
    e1 = (i1 - N_GROUPS).astype(F32)
    e2 = (i2 - N_GROUPS).astype(F32)
    return jnp.where(lane == 0, e1, jnp.where(lane == 1, e2,
                     jnp.where(lane == 2, w1, jnp.where(lane == 3, w2, 0.0))))


def _mid_kernel(x_ref, ycn_ref, yan_ref, wout_ref, gx_ref, wxq_ref, km_ref, vm_ref, wxo_ref,
                gmoe_ref, wrt_ref, brt_ref, x2_ref, xp_ref, route_ref):
    d = x_ref.shape[1]
    cw = ycn_ref.shape[1]
    xd = d // N_X_HEADS
    x1 = (x_ref[...]
          + jnp.dot(ycn_ref[...], wout_ref[:cw, :], preferred_element_type=F32)
          + jnp.dot(yan_ref[...], wout_ref[cw:, :], preferred_element_type=F32))
    h2 = _rms(x1, gx_ref[...]).astype(BF16)
    q = (jnp.dot(h2, wxq_ref[...], preferred_element_type=F32) * (1.0 / math.sqrt(xd))).astype(BF16)
    heads = []
    for hh in range(N_X_HEADS):
        sl = slice(hh * xd, (hh + 1) * xd)
        s = lax.dot_general(q[:, sl], km_ref[:, sl], (((1,), (1,)), ((), ())),
                            preferred_element_type=F32)
        e = jnp.exp(s - jnp.max(s, axis=-1, keepdims=True))
        p = (e / jnp.sum(e, axis=-1, keepdims=True)).astype(BF16)
        heads.append(jnp.dot(p, vm_ref[:, sl], preferred_element_type=F32).astype(BF16))
    o = jnp.concatenate(heads, axis=-1)
    x2 = x1 + jnp.dot(o, wxo_ref[...], preferred_element_type=F32)
    x2_ref[...] = x2
    xn = _rms(x2, gmoe_ref[...])
    logits = jnp.dot(xn, wrt_ref[...], preferred_element_type=F32,
                     precision=lax.Precision.HIGHEST) + brt_ref[...]
    route_ref[...] = _route(logits)
    bits = pltpu.bitcast(xn.astype(BF16).astype(F32), jnp.uint32)
    half = d // 2
    xp_ref[...] = (bits[:, :half] >> 16) | (bits[:, half:] & jnp.uint32(0xFFFF0000))


def _mid(x2d, ycn, yan, w_out, g_xattn, w_xq, kmem, vmem, w_xo, g_moe, w_rt, b_rt, *, seq, mem_len):
    n, d = x2d.shape
    cw = ycn.shape[1]
    tm = ROW_TILE
    per_seq = seq // tm
    row = lambda c: pl.BlockSpec((tm, c), lambda i: (i, 0))
    memblk = pl.BlockSpec((mem_len, d), lambda i: (i // per_seq, 0))
    return pl.pallas_call(
        _mid_kernel,
        grid=(n // tm,),
        in_specs=[row(d), row(cw), row(yan.shape[1]), _resident((d, d)), _resident((1, d)),
                  _resident((d, d)), memblk, memblk, _resident((d, d)), _resident((1, d)),
                  _resident((d, LANES)), _resident((1, LANES))],
        out_specs=[row(d), row(d // 2), row(LANES)],
        out_shape=[jax.ShapeDtypeStruct((n, d), F32), jax.ShapeDtypeStruct((n, d // 2), jnp.uint32),
                   jax.ShapeDtypeStruct((n, LANES), F32)],
        compiler_params=_params(1),
        name="mid",
    )(x2d, ycn, yan, w_out, g_xattn, w_xq, kmem, vmem, w_xo, g_moe, w_rt, b_rt)


def _gather_rows(idx_ref, base, n_rows, src_hbm, dst_buf, sem):
    def body(r, carry):
        t = idx_ref[base + r]
        pltpu.make_async_copy(src_hbm.at[pl.ds(t, 1)], dst_buf.at[pl.ds(r, 1)], sem).start()
        return carry
    lax.fori_loop(0, n_rows, body, 0, unroll=8)


def _wait_rows(dst_buf, sem):
    pltpu.make_async_copy(dst_buf, dst_buf, sem).wait()


def _experts_kernel(be_ref, na_ref, tok_ref, xp_hbm, gw_ref, wg_ref, wu_ref, wd_ref, y_ref,
                    xbuf, sem, wgb, wub, wdb):
    i = pl.program_id(0)
    na = na_ref[0]
    slot = i % 2
    blk = xbuf.shape[1]
    half = xbuf.shape[2]

    @pl.when(i == 0)
    def _():
        _gather_rows(tok_ref, 0, blk, xp_hbm, xbuf.at[0], sem.at[0])

    @pl.when(i + 1 < na)
    def _():
        _gather_rows(tok_ref, (i + 1) * blk, blk, xp_hbm, xbuf.at[1 - slot], sem.at[1 - slot])

    @pl.when(i < na)
    def _():
        _wait_rows(xbuf.at[slot], sem.at[slot])
        new_expert = jnp.logical_or(i == 0, be_ref[i] != be_ref[jnp.maximum(i - 1, 0)])

        @pl.when(new_expert)
        def _():
            wgb[...] = wg_ref[...].astype(BF16)
            wub[...] = wu_ref[...].astype(BF16)
            wdb[...] = wd_ref[...].astype(BF16)

        w = xbuf[slot]
        xa = pltpu.bitcast(w << 16, F32).astype(BF16)
        xb = pltpu.bitcast(w & jnp.uint32(0xFFFF0000), F32).astype(BF16)
        g = (jnp.dot(xa, wgb[:half, :], preferred_element_type=F32)
             + jnp.dot(xb, wgb[half:, :], preferred_element_type=F32))
        u = (jnp.dot(xa, wub[:half, :], preferred_element_type=F32)
             + jnp.dot(xb, wub[half:, :], preferred_element_type=F32))
        act = (g * jax.nn.sigmoid(g) * u).astype(BF16)
        y_ref[...] = jnp.dot(act, wdb[...], preferred_element_type=F32) * gw_ref[...]

    @pl.when(i >= na)
    def _():
        y_ref[...] = jnp.zeros_like(y_ref)


def _experts(blk_expert, n_active, buf_tok, xp, buf_w, w_gate, w_up, w_down):
    n_blk = blk_expert.shape[0]
    half = xp.shape[1]
    d = 2 * half
    de = w_gate.shape[2]
    blk = MOE_BLOCK
    grid_spec = pltpu.PrefetchScalarGridSpec(
        num_scalar_prefetch=3,
        grid=(n_blk,),
        in_specs=[pl.BlockSpec(memory_space=pl.ANY),
                  pl.BlockSpec((blk, 1), lambda i, be, na, tok: (jnp.minimum(i, na[0] - 1), 0)),
                  pl.BlockSpec((None, d, de), lambda i, be, na, tok: (be[i], 0, 0)),
                  pl.BlockSpec((None, d, de), lambda i, be, na, tok: (be[i], 0, 0)),
                  pl.BlockSpec((None, de, d), lambda i, be, na, tok: (be[i], 0, 0))],
        out_specs=pl.BlockSpec((blk, d), lambda i, be, na, tok: (i, 0)),
        scratch_shapes=[pltpu.VMEM((2, blk, half), jnp.uint32), pltpu.SemaphoreType.DMA((2,)),
                        pltpu.VMEM((d, de), BF16), pltpu.VMEM((d, de), BF16), pltpu.VMEM((de, d), BF16)],
    )
    return pl.pallas_call(
        _experts_kernel,
        grid_spec=grid_spec,
        out_shape=jax.ShapeDtypeStruct((n_blk * blk, d), F32),
        compiler_params=_params(1),
        name="experts",
    )(blk_expert, n_active, buf_tok, xp, buf_w, w_gate, w_up, w_down)


def _combine_kernel(pos_ref, x2_ref, g_ref, y_hbm, out_ref, ybuf, sem):
    i = pl.program_id(0)
    n = pl.num_programs(0)
    slot = i % 2
    tm = x2_ref.shape[0]
    rows = 2 * tm

    @pl.when(i == 0)
    def _():
        _gather_rows(pos_ref, 0, rows, y_hbm, ybuf.at[0], sem.at[0])

    @pl.when(i + 1 < n)
    def _():
        _gather_rows(pos_ref, (i + 1) * rows, rows, y_hbm, ybuf.at[1 - slot], sem.at[1 - slot])

    _wait_rows(ybuf.at[slot], sem.at[slot])
    x = x2_ref[...] + (ybuf[slot, :tm, :] + ybuf[slot, tm:, :])
    out_ref[...] = _rms(x, g_ref[...])


def _combine(pos, x2, g_final, y):
    n, d = x2.shape
    tm = ROW_TILE
    grid_spec = pltpu.PrefetchScalarGridSpec(
        num_scalar_prefetch=1,
        grid=(n // tm,),
        in_specs=[pl.BlockSpec((tm, d), lambda i, pos: (i, 0)),
                  pl.BlockSpec((1, d), lambda i, pos: (0, 0)),
                  pl.BlockSpec(memory_space=pl.ANY)],
        out_specs=pl.BlockSpec((tm, d), lambda i, pos: (i, 0)),
        scratch_shapes=[pltpu.VMEM((2, 2 * tm, d), F32), pltpu.SemaphoreType.DMA((2,))],
    )
    return pl.pallas_call(
        _combine_kernel,
        grid_spec=grid_spec,
        out_shape=jax.ShapeDtypeStruct((n, d), F32),
        compiler_params=_params(1),
        name="combine",
    )(pos, x2, g_final, y)


def _dispatch(expert_id, gate_w):
    n_tok = expert_id.shape[0]
    n_slot = n_tok * TOP_K
    flat_e = expert_id.reshape(-1)
    flat_w = gate_w.reshape(-1)
    order = jnp.argsort(flat_e, stable=True)
    s_e = flat_e[order]
    counts = jnp.zeros((N_EXPERTS,), jnp.int32).at[flat_e].add(1)
    padded = ((counts + MOE_BLOCK - 1) // MOE_BLOCK) * MOE_BLOCK
    pad_end = jnp.cumsum(padded)
    pad_start = pad_end - padded
    start = jnp.cumsum(counts) - counts
    dest = pad_start[s_e] + (jnp.arange(n_slot, dtype=jnp.int32) - start[s_e])
    n_blk = n_slot // MOE_BLOCK + N_EXPERTS
    length = n_blk * MOE_BLOCK
    buf_tok = jnp.zeros((length,), jnp.int32).at[dest].set((order // TOP_K).astype(jnp.int32))
    buf_w = jnp.zeros((length,), F32).at[dest].set(flat_w[order])
    pos = jnp.zeros((n_slot,), jnp.int32).at[order].set(dest.astype(jnp.int32))
    n_active = (pad_end[-1] // MOE_BLOCK).astype(jnp.int32)
    blk_ids = jnp.arange(n_blk, dtype=jnp.int32)
    blk_expert = jnp.minimum(jnp.searchsorted(pad_end, blk_ids * MOE_BLOCK, side="right"),
                             N_EXPERTS - 1).astype(jnp.int32)
    blk_expert = jnp.where(blk_ids < n_active, blk_expert, blk_expert[n_active - 1])
    return blk_expert, n_active.reshape(1), buf_tok, buf_w.reshape(length, 1), pos


def _rotate_half_cols(w):
    half = w.shape[-1] // 2
    return jnp.concatenate([-w[..., half:], w[..., :half]], axis=-1)


def _layer(x, mem, positions, g_mix, w_in, w_conv, g_q_a, w_q_up, g_kv_a, w_kv_up, g_conv_out,
           g_attn_out, w_out, g_xattn, g_mem, w_xq, w_xk, w_xv, w_xo, g_moe, w_group, b_group,
           w_router, b_router, w_gate, w_up, w_down):
    batch, seq, d = x.shape
    mem_len = mem.shape[1]
    n = batch * seq
    cw = w_conv.shape[1]
    ql = g_q_a.shape[0]
    kvl = g_kv_a.shape[0]
    c_rope = 3 * cw + ql + kvl
    row2 = lambda g: g.reshape(1, -1)

    w_kr = w_in[:, c_rope:c_rope + QK_ROPE]
    w_in_ext = jnp.concatenate([w_in[:, :c_rope + QK_ROPE], _rotate_half_cols(w_kr)], axis=1).astype(BF16)
    inv_freq = 1.0 / (ROPE_THETA ** (jnp.arange(0, QK_ROPE, 2, dtype=F32) / QK_ROPE))
    freq = jnp.tile(inv_freq, LANES // inv_freq.shape[0]).reshape(1, LANES)

    ycn, cqn, ckvn, kr, cs = _in_proj(
        x.reshape(n, d), positions.reshape(n, 1), row2(g_mix), w_in_ext, w_conv, row2(g_q_a),
        row2(g_kv_a), row2(g_conv_out), freq, seq=seq, cw=cw, ql=ql, kvl=kvl)

    wq = w_q_up.reshape(ql, N_MLA_HEADS, QK_NOPE + QK_ROPE).transpose(1, 0, 2)
    wq_nope, wq_rope = wq[..., :QK_NOPE], wq[..., QK_NOPE:]
    z64 = jnp.zeros((N_MLA_HEADS, ql, QK_ROPE), F32)
    wqa = jnp.concatenate([wq_nope, wq_rope, z64], axis=-1).astype(BF16)
    wqb = jnp.concatenate([jnp.zeros_like(wq_nope), _rotate_half_cols(wq_rope), z64], axis=-1).astype(BF16)
    wkv = w_kv_up.reshape(kvl, N_MLA_HEADS, QK_NOPE + V_HEAD).transpose(1, 0, 2).astype(BF16)

    yan = _attention(cqn, cs, ckvn, kr, wqa, wqb, wkv, row2(g_attn_out), batch=batch, seq=seq)

    w_xkv = jnp.concatenate([w_xk, w_xv], axis=1).astype(BF16)
    kmem, vmem = _mem_kv(mem.reshape(batch * mem_len, d), row2(g_mem), w_xkv, mem_len=mem_len)

    pad = LANES - N_GROUPS - N_EXPERTS
    w_rt = jnp.concatenate([w_group, w_router, jnp.zeros((d, pad), F32)], axis=1)
    b_rt = jnp.concatenate([b_group, b_router, jnp.zeros((pad,), F32)]).reshape(1, LANES)
    x2, xp, route = _mid(x.reshape(n, d), ycn, yan, w_out.astype(BF16), row2(g_xattn), w_xq.astype(BF16),
                         kmem, vmem, w_xo.astype(BF16), row2(g_moe), w_rt, b_rt, seq=seq, mem_len=mem_len)

    expert_id = route[:, :TOP_K].astype(jnp.int32)
    gate_w = route[:, TOP_K:2 * TOP_K]
    blk_expert, n_active, buf_tok, buf_w, pos = _dispatch(expert_id, gate_w)
    y = _experts(blk_expert, n_active, buf_tok, xp, buf_w, w_gate, w_up, w_down)
    pos_tiles = pos.reshape(n // ROW_TILE, ROW_TILE, TOP_K).transpose(0, 2, 1).reshape(-1)
    return x2, pos_tiles, y


def kernel(x, mem, positions, g_mix, w_in, w_conv, g_q_a, w_q_up, g_kv_a, w_kv_up, g_conv_out, g_attn_out, w_out, g_xattn, g_mem, w_xq, w_xk, w_xv, w_xo, g_moe, w_group, b_group, w_router, b_router, w_gate, w_up, w_down, g_final):
    depth = g_mix.shape[0]
    assert depth == 1, "the combine stage applies the final norm, so exactly one layer is supported"
    shape = x.shape
    x2, pos_tiles, y = _layer(
        x, mem, positions, g_mix[0], w_in[0], w_conv[0], g_q_a[0], w_q_up[0], g_kv_a[0], w_kv_up[0],
        g_conv_out[0], g_attn_out[0], w_out[0], g_xattn[0], g_mem[0], w_xq[0], w_xk[0], w_xv[0],
        w_xo[0], g_moe[0], w_group[0], b_group[0], w_router[0], b_router[0], w_gate[0], w_up[0], w_down[0])
    out = _combine(pos_tiles, x2, g_final.reshape(1, -1), y)
    return out.reshape(shape)
```

```python
import functools
import math

import jax
import jax.numpy as jnp
from jax import lax
from jax.experimental import pallas as pl
from jax.experimental.pallas import tpu as pltpu

F32 = jnp.float32
BF16 = jnp.bfloat16

RMS_EPS = 1e-6
N_MLA_HEADS = 8
QK_NOPE = 128
QK_ROPE = 64
V_HEAD = 128
ROPE_THETA = 10000.0
MASK_VALUE = -1e30
N_X_HEADS = 4
N_GROUPS = 8
EXPERTS_PER_GROUP = 8
N_EXPERTS = N_GROUPS * EXPERTS_PER_GROUP
TOP_K = 2
MOE_BLOCK = 256

LANES = 128
VMEM_LIMIT_BYTES = 60000 * 1024

IN_PROJ_TILE = 512
ROW_TILE = 256
ATTN_TILE = 512


def _rms(x, g):
    ms = jnp.mean(x * x, axis=-1, keepdims=True)
    return x * lax.rsqrt(ms + RMS_EPS) * g


def _resident(shape):
    nd = len(shape)
    return pl.BlockSpec(shape, lambda *_: (0,) * nd, pipeline_mode=pl.Buffered(1))


def _params(n_axes):
    return pltpu.CompilerParams(dimension_semantics=("arbitrary",) * n_axes,
                                vmem_limit_bytes=VMEM_LIMIT_BYTES)


def _in_proj_kernel(x_ref, pos_ref, gmix_ref, w_ref, wconv_ref, gq_ref, gkv_ref, gco_ref, freq_ref,
                    ycn_ref, cqn_ref, ckvn_ref, kr_ref, cs_ref, carry_ref,
                    *, tiles_per_seq, cw, ql, kvl):
    i = pl.program_id(0)
    tm = x_ref.shape[0]
    xn = _rms(x_ref[...], gmix_ref[...]).astype(BF16)

    def mm(lo, hi):
        return jnp.dot(xn, w_ref[:, lo:hi], preferred_element_type=F32)

    zb, zc, zh = mm(0, cw), mm(cw, 2 * cw), mm(2 * cw, 3 * cw)
    u = zc * zh

    @pl.when(i % tiles_per_seq == 0)
    def _():
        carry_ref[...] = jnp.zeros_like(carry_ref)

    prev = carry_ref[...]
    carry_ref[...] = u[tm - 8:, :]
    row = lax.broadcasted_iota(jnp.int32, (tm, 1), 0)
    u1 = jnp.where(row == 0, prev[7:8, :], pltpu.roll(u, 1, axis=0))
    u2 = jnp.where(row == 0, prev[6:7, :],
                   jnp.where(row == 1, prev[7:8, :], pltpu.roll(u, 2, axis=0)))
    wc = wconv_ref[...]
    conv = wc[0:1, :] * u2 + wc[1:2, :] * u1 + wc[2:3, :] * u
    ycn_ref[...] = _rms(zb * conv, gco_ref[...]).astype(BF16)

    c3 = 3 * cw
    cqn_ref[...] = _rms(mm(c3, c3 + ql), gq_ref[...]).astype(BF16)
    ckvn_ref[...] = _rms(mm(c3 + ql, c3 + ql + kvl), gkv_ref[...]).astype(BF16)

    zk = mm(c3 + ql + kvl, c3 + ql + kvl + LANES)
    ang = pos_ref[...].astype(F32) * freq_ref[...]
    cos, sin = jnp.cos(ang), jnp.sin(ang)
    lane = lax.broadcasted_iota(jnp.int32, (tm, LANES), 1)
    lo = lane < QK_ROPE
    cos_sin = jnp.where(lo, cos, sin)
    t = zk * cos_sin
    kr = t + pltpu.roll(t, QK_ROPE, axis=1)
    kr_ref[...] = jnp.where(lo, kr, 0.0).astype(BF16)
    cs_ref[...] = jnp.concatenate([cos_sin, jnp.where(lo, sin, cos)], axis=-1)


def _in_proj(x2d, pos2d, g_mix, w_in_ext, w_conv, g_q_a, g_kv_a, g_conv_out, freq, *, seq, cw, ql, kvl):
    n, d = x2d.shape
    tm = IN_PROJ_TILE
    wcols = w_in_ext.shape[1]
    kern = functools.partial(_in_proj_kernel, tiles_per_seq=seq // tm, cw=cw, ql=ql, kvl=kvl)
    row = lambda c: pl.BlockSpec((tm, c), lambda i: (i, 0))
    return pl.pallas_call(
        kern,
        grid=(n // tm,),
        in_specs=[row(d), row(1), _resident((1, d)), _resident((d, wcols)), _resident((3, cw)),
                  _resident((1, ql)), _resident((1, kvl)), _resident((1, cw)), _resident((1, LANES))],
        out_specs=[row(cw), row(ql), row(kvl), row(LANES), row(2 * LANES)],
        out_shape=[jax.ShapeDtypeStruct((n, cw), BF16), jax.ShapeDtypeStruct((n, ql), BF16),
                   jax.ShapeDtypeStruct((n, kvl), BF16), jax.ShapeDtypeStruct((n, LANES), BF16),
                   jax.ShapeDtypeStruct((n, 2 * LANES), F32)],
        scratch_shapes=[pltpu.VMEM((8, cw), F32)],
        compiler_params=_params(1),
        name="in_proj",
    )(x2d, pos2d, g_mix, w_in_ext, w_conv, g_q_a, g_kv_a, g_conv_out, freq)


def _attn_kernel(cqn_ref, cs_ref, ckvn_ref, kr_ref, wqa_ref, wqb_ref, wk_ref, wvt_ref, gao_ref,
                 out_ref, k_scr, vt_scr, o_scr, m_scr, l_scr, acc_scr, *, scale):
    qi = pl.program_id(1)
    h = pl.program_id(2)
    n_heads = pl.num_programs(2)
    tq = cqn_ref.shape[0]
    tk = tq
    nt = (((1,), (1,)), ((), ()))

    @pl.when(qi == 0)
    def _():
        ckv = ckvn_ref[...]
        k_scr[h, :, 0:QK_NOPE] = jnp.dot(ckv, wk_ref[...], preferred_element_type=F32).astype(BF16)
        k_scr[h, :, QK_NOPE:2 * QK_NOPE] = kr_ref[...]
        vt = lax.dot_general(wvt_ref[...], ckv, nt, preferred_element_type=F32)
        for kb in range(vt_scr.shape[1]):
            vt_scr[h, kb] = vt[:, kb * tk:(kb + 1) * tk].astype(BF16)

    cq = cqn_ref[...]
    cs = cs_ref[...]
    ones = jnp.ones((tq, LANES), F32)
    m1 = jnp.concatenate([ones, cs[:, :LANES]], axis=-1) * scale
    m2 = jnp.concatenate([ones, cs[:, LANES:]], axis=-1) * scale
    q = (jnp.dot(cq, wqa_ref[...], preferred_element_type=F32) * m1
         + jnp.dot(cq, wqb_ref[...], preferred_element_type=F32) * m2).astype(BF16)

    m_scr[...] = jnp.full_like(m_scr, -jnp.inf)
    l_scr[...] = jnp.zeros_like(l_scr)
    acc_scr[...] = jnp.zeros_like(acc_scr)

    def step(kb, masked):
        start = pl.multiple_of(kb * tk, tk)
        k = k_scr[h, pl.ds(start, tk), :]
        s = lax.dot_general(k, q, nt, preferred_element_type=F32)
        if masked:
            kpos = lax.broadcasted_iota(jnp.int32, (tk, tq), 0)
            qpos = lax.broadcasted_iota(jnp.int32, (tk, tq), 1)
            s = jnp.where(kpos <= qpos, s, MASK_VALUE)
        m_prev = m_scr[...]
        m_new = jnp.maximum(m_prev, jnp.max(s, axis=0, keepdims=True))
        alpha = jnp.exp(m_prev - m_new)
        p = jnp.exp(s - m_new)
        l_scr[...] = alpha * l_scr[...] + jnp.sum(p, axis=0, keepdims=True)
        pv = jnp.dot(vt_scr[h, kb], p.astype(BF16), preferred_element_type=F32)
        acc_scr[...] = alpha * acc_scr[...] + pv
        m_scr[...] = m_new

    def body(kb, carry):
        step(kb, False)
        return carry

    lax.fori_loop(0, qi, body, 0)
    step(qi, True)
    o_scr[h] = (acc_scr[...] / l_scr[...]).T

    @pl.when(h == n_heads - 1)
    def _():
        o = jnp.concatenate([o_scr[j] for j in range(N_MLA_HEADS)], axis=-1)
        out_ref[...] = _rms(o, gao_ref[...]).astype(BF16)


def _attention(cqn, cs, ckvn, kr, wqa, wqb, wk, wvt, g_attn_out, *, batch, seq):
    n, ql = cqn.shape
    kvl = ckvn.shape[1]
    tq = ATTN_TILE
    nq = seq // tq
    aw = N_MLA_HEADS * V_HEAD
    kern = functools.partial(_attn_kernel, scale=1.0 / math.sqrt(QK_NOPE + QK_ROPE))
    qrow = lambda c: pl.BlockSpec((tq, c), lambda b, qi, h: (b * nq + qi, 0))
    brow = lambda c: pl.BlockSpec((seq, c), lambda b, qi, h: (b, 0))
    head = lambda r, c: pl.BlockSpec((None, r, c), lambda b, qi, h: (h, 0, 0))
    return pl.pallas_call(
        kern,
        grid=(batch, nq, N_MLA_HEADS),
        in_specs=[qrow(ql), qrow(2 * LANES), brow(kvl), brow(LANES),
                  head(ql, 2 * QK_NOPE), head(ql, 2 * QK_NOPE), head(kvl, QK_NOPE), head(V_HEAD, kvl),
                  pl.BlockSpec((1, aw), lambda b, qi, h: (0, 0))],
        out_specs=qrow(aw),
        out_shape=jax.ShapeDtypeStruct((n, aw), BF16),
        scratch_shapes=[pltpu.VMEM((N_MLA_HEADS, seq, 2 * QK_NOPE), BF16),
                        pltpu.VMEM((N_MLA_HEADS, nq, V_HEAD, tq), BF16),
                        pltpu.VMEM((N_MLA_HEADS, tq, V_HEAD), F32),
                        pltpu.VMEM((1, tq), F32), pltpu.VMEM((1, tq), F32),
                        pltpu.VMEM((V_HEAD, tq), F32)],
        compiler_params=_params(3),
        name="mla_attention",
    )(cqn, cs, ckvn, kr, wqa, wqb, wk, wvt, g_attn_out)


def _mem_kv_kernel(mem_ref, g_ref, w_ref, k_ref, v_ref):
    d = mem_ref.shape[1]
    mn = _rms(mem_ref[...], g_ref[...]).astype(BF16)
    k_ref[...] = jnp.dot(mn, w_ref[:, :d], preferred_element_type=F32).astype(BF16)
    v_ref[...] = jnp.dot(mn, w_ref[:, d:], preferred_element_type=F32).astype(BF16)


def _mem_kv(mem2d, g_mem, w_xkv, *, mem_len):
    n, d = mem2d.shape
    row = pl.BlockSpec((mem_len, d), lambda i: (i, 0))
    return pl.pallas_call(
        _mem_kv_kernel,
        grid=(n // mem_len,),
        in_specs=[row, _resident((1, d)), _resident((d, 2 * d))],
        out_specs=[row, row],
        out_shape=[jax.ShapeDtypeStruct((n, d), BF16)] * 2,
        compiler_params=_params(1),
        name="mem_kv",
    )(mem2d, g_mem, w_xkv)


def _route(logits):
    tm = logits.shape[0]
    lane = lax.broadcasted_iota(jnp.int32, (tm, LANES), 1)
    big = jnp.int32(1 << 20)
    ninf = -jnp.inf
    gl = jnp.where(lane < N_GROUPS, logits, ninf)
    gm = jnp.max(gl, axis=-1, keepdims=True)
    gidx = jnp.min(jnp.where(gl == gm, lane, big), axis=-1, keepdims=True)
    g_p = 1.0 / jnp.sum(jnp.exp(gl - gm), axis=-1, keepdims=True)
    first = N_GROUPS + gidx * EXPERTS_PER_GROUP
    el = jnp.where(lane >= first, jnp.where(lane < first + EXPERTS_PER_GROUP, logits, ninf), ninf)
    top1 = jnp.max(el, axis=-1, keepdims=True)
    i1 = jnp.min(jnp.where(el == top1, lane, big), axis=-1, keepdims=True)
    el2 = jnp.where(lane == i1, ninf, el)
    top2 = jnp.max(el2, axis=-1, keepdims=True)
    i2 = jnp.min(jnp.where(el2 == top2, lane, big), axis=-1, keepdims=True)
    r = jnp.exp(top2 - top1)
    w1 = g_p / (1.0 + r)
    w2 = g_p * r / (1.0 + r)
    e1 = (i1 - N_GROUPS).astype(F32)
    e2 = (i2 - N_GROUPS).astype(F32)
    return jnp.where(lane == 0, e1, jnp.where(lane == 1, e2,
                     jnp.where(lane == 2, w1, jnp.where(lane == 3, w2, 0.0))))


def _mid_kernel(x_ref, ycn_ref, yan_ref, wout_ref, gx_ref, wxq_ref, km_ref, vm_ref, wxo_ref,
                gmoe_ref, wrh_ref, wrl_ref, brt_ref, x2_ref, xp_ref, route_ref):
    d = x_ref.shape[1]
    cw = ycn_ref.shape[1]
    xd = d // N_X_HEADS
    x1 = (x_ref[...]
          + jnp.dot(ycn_ref[...], wout_ref[:cw, :], preferred_element_type=F32)
          + jnp.dot(yan_ref[...], wout_ref[cw:, :], preferred_element_type=F32))
    h2 = _rms(x1, gx_ref[...]).astype(BF16)
    q = (jnp.dot(h2, wxq_ref[...], preferred_element_type=F32) * (1.0 / math.sqrt(xd))).astype(BF16)
    heads = []
    for hh in range(N_X_HEADS):
        sl = slice(hh * xd, (hh + 1) * xd)
        s = lax.dot_general(q[:, sl], km_ref[:, sl], (((1,), (1,)), ((), ())),
                            preferred_element_type=F32)
        e = jnp.exp(s - jnp.max(s, axis=-1, keepdims=True))
        p = (e / jnp.sum(e, axis=-1, keepdims=True)).astype(BF16)
        heads.append(jnp.dot(p, vm_ref[:, sl], preferred_element_type=F32).astype(BF16))
    o = jnp.concatenate(heads, axis=-1)
    x2 = x1 + jnp.dot(o, wxo_ref[...], preferred_element_type=F32)
    x2_ref[...] = x2
    xn = _rms(x2, gmoe_ref[...])
    xn_hi = xn.astype(BF16)
    xn_lo = (xn - xn_hi.astype(F32)).astype(BF16)
    logits = (jnp.dot(xn_hi, wrh_ref[...], preferred_element_type=F32)
              + (jnp.dot(xn_lo, wrh_ref[...], preferred_element_type=F32)
                 + jnp.dot(xn_hi, wrl_ref[...], preferred_element_type=F32))) + brt_ref[...]
    route_ref[...] = _route(logits)
    bits = pltpu.bitcast(xn_hi.astype(F32), jnp.uint32)
    half = d // 2
    xp_ref[...] = (bits[:, :half] >> 16) | (bits[:, half:] & jnp.uint32(0xFFFF0000))


def _mid(x2d, ycn, yan, w_out, g_xattn, w_xq, kmem, vmem, w_xo, g_moe, w_rt_hi, w_rt_lo, b_rt,
         *, seq, mem_len):
    n, d = x2d.shape
    cw = ycn.shape[1]
    tm = ROW_TILE
    per_seq = seq // tm
    row = lambda c: pl.BlockSpec((tm, c), lambda i: (i, 0))
    memblk = pl.BlockSpec((mem_len, d), lambda i: (i // per_seq, 0))
    return pl.pallas_call(
        _mid_kernel,
        grid=(n // tm,),
        in_specs=[row(d), row(cw), row(yan.shape[1]), _resident((d, d)), _resident((1, d)),
                  _resident((d, d)), memblk, memblk, _resident((d, d)), _resident((1, d)),
                  _resident((d, LANES)), _resident((d, LANES)), _resident((1, LANES))],
        out_specs=[row(d), row(d // 2), row(LANES)],
        out_shape=[jax.ShapeDtypeStruct((n, d), F32), jax.ShapeDtypeStruct((n, d // 2), jnp.uint32),
                   jax.ShapeDtypeStruct((n, LANES), F32)],
        compiler_params=_params(1),
        name="mid",
    )(x2d, ycn, yan, w_out, g_xattn, w_xq, kmem, vmem, w_xo, g_moe, w_rt_hi, w_rt_lo, b_rt)


def _gather_rows(idx_ref, base, n_rows, src_hbm, dst_buf, sem, inline=False):
    def start_row(r):
        t = idx_ref[base + r]
        pltpu.make_async_copy(src_hbm.at[pl.ds(t, 1)], dst_buf.at[pl.ds(r, 1)], sem).start()

    if inline:
        for r in range(n_rows):
            start_row(r)
    else:
        def body(r, carry):
            start_row(r)
            return carry
        lax.fori_loop(0, n_rows, body, 0, unroll=8)


def _wait_rows(dst_buf, sem):
    pltpu.make_async_copy(dst_buf, dst_buf, sem).wait()


def _experts_kernel(be_ref, na_ref, tok_ref, xp_hbm, gw_ref, wg_ref, wu_ref, wd_ref, y_ref,
                    xbuf, sem, wgb, wub, wdb):
    i = pl.program_id(0)
    na = na_ref[0]
    slot = i % 2
    blk = xbuf.shape[1]
    half = xbuf.shape[2]

    @pl.when(i == 0)
    def _():
        _gather_rows(tok_ref, 0, blk, xp_hbm, xbuf.at[0], sem.at[0])

    @pl.when(i <= na)
    def _():
        _wait_rows(xbuf.at[slot], sem.at[slot])

    @pl.when(i < na)
    def _():
        new_expert = jnp.logical_or(i == 0, be_ref[i] != be_ref[jnp.maximum(i - 1, 0)])

        @pl.when(new_expert)
        def _():
            wgb[...] = wg_ref[...].astype(BF16)
            wub[...] = wu_ref[...].astype(BF16)
            wdb[...] = wd_ref[...].astype(BF16)

        _gather_rows(tok_ref, (i + 1) * blk, blk, xp_hbm, xbuf.at[1 - slot], sem.at[1 - slot], inline=True)
        w = xbuf[slot]
        xa = pltpu.bitcast(w << 16, F32).astype(BF16)
        xb = pltpu.bitcast(w & jnp.uint32(0xFFFF0000), F32).astype(BF16)
        g = (jnp.dot(xa, wgb[:half, :], preferred_element_type=F32)
             + jnp.dot(xb, wgb[half:, :], preferred_element_type=F32))
        u = (jnp.dot(xa, wub[:half, :], preferred_element_type=F32)
             + jnp.dot(xb, wub[half:, :], preferred_element_type=F32))
        act = (g * jax.nn.sigmoid(g) * u).astype(BF16)
        y_ref[...] = jnp.dot(act, wdb[...], preferred_element_type=F32) * gw_ref[...]

    @pl.when(i >= na)
    def _():
        y_ref[...] = jnp.zeros_like(y_ref)


def _experts(blk_expert, n_active, buf_tok, xp, buf_w, w_gate, w_up, w_down):
    n_blk = blk_expert.shape[0]
    half = xp.shape[1]
    d = 2 * half
    de = w_gate.shape[2]
    blk = MOE_BLOCK
    grid_spec = pltpu.PrefetchScalarGridSpec(
        num_scalar_prefetch=3,
        grid=(n_blk,),
        in_specs=[pl.BlockSpec(memory_space=pl.ANY),
                  pl.BlockSpec((blk, 1), lambda i, be, na, tok: (jnp.minimum(i, na[0] - 1), 0)),
                  pl.BlockSpec((None, d, de), lambda i, be, na, tok: (be[i], 0, 0)),
                  pl.BlockSpec((None, d, de), lambda i, be, na, tok: (be[i], 0, 0)),
                  pl.BlockSpec((None, de, d), lambda i, be, na, tok: (be[i], 0, 0))],
        out_specs=pl.BlockSpec((blk, d), lambda i, be, na, tok: (i, 0)),
        scratch_shapes=[pltpu.VMEM((2, blk, half), jnp.uint32), pltpu.SemaphoreType.DMA((2,)),
                        pltpu.VMEM((d, de), BF16), pltpu.VMEM((d, de), BF16), pltpu.VMEM((de, d), BF16)],
    )
    return pl.pallas_call(
        _experts_kernel,
        grid_spec=grid_spec,
        out_shape=jax.ShapeDtypeStruct((n_blk * blk, d), F32),
        compiler_params=_params(1),
        name="experts",
    )(blk_expert, n_active, buf_tok, xp, buf_w, w_gate, w_up, w_down)


def _combine_kernel(pos_ref, x2_ref, g_ref, y_hbm, out_ref, ybuf, sem):
    i = pl.program_id(0)
    n = pl.num_programs(0)
    slot = i % 2
    tm = x2_ref.shape[0]
    rows = 2 * tm

    @pl.when(i == 0)
    def _():
        _gather_rows(pos_ref, 0, rows, y_hbm, ybuf.at[0], sem.at[0])

    @pl.when(i + 1 < n)
    def _():
        _gather_rows(pos_ref, (i + 1) * rows, rows, y_hbm, ybuf.at[1 - slot], sem.at[1 - slot])

    _wait_rows(ybuf.at[slot], sem.at[slot])
    x = x2_ref[...] + (ybuf[slot, :tm, :] + ybuf[slot, tm:, :])
    out_ref[...] = _rms(x, g_ref[...])


def _combine(pos, x2, g_final, y):
    n, d = x2.shape
    tm = ROW_TILE
    grid_spec = pltpu.PrefetchScalarGridSpec(
        num_scalar_prefetch=1,
        grid=(n // tm,),
        in_specs=[pl.BlockSpec((tm, d), lambda i, pos: (i, 0)),
                  pl.BlockSpec((1, d), lambda i, pos: (0, 0)),
                  pl.BlockSpec(memory_space=pl.ANY)],
        out_specs=pl.BlockSpec((tm, d), lambda i, pos: (i, 0)),
        scratch_shapes=[pltpu.VMEM((2, 2 * tm, d), F32), pltpu.SemaphoreType.DMA((2,))],
    )
    return pl.pallas_call(
        _combine_kernel,
        grid_spec=grid_spec,
        out_shape=jax.ShapeDtypeStruct((n, d), F32),
        compiler_params=_params(1),
        name="combine",
    )(pos, x2, g_final, y)


def _dispatch(expert_id, gate_w):
    n_tok = expert_id.shape[0]
    n_slot = n_tok * TOP_K
    flat_e = expert_id.reshape(-1)
    flat_w = gate_w.reshape(-1)
    order = jnp.argsort(flat_e, stable=True).astype(jnp.int32)
    s_e = flat_e[order]
    start_ext = jnp.searchsorted(s_e, jnp.arange(N_EXPERTS + 1, dtype=jnp.int32), side="left").astype(jnp.int32)
    start = start_ext[:-1]
    counts = start_ext[1:] - start
    padded = ((counts + MOE_BLOCK - 1) // MOE_BLOCK) * MOE_BLOCK
    pad_end = jnp.cumsum(padded)
    pad_start = pad_end - padded
    n_blk = n_slot // MOE_BLOCK + N_EXPERTS
    length = n_blk * MOE_BLOCK
    n_active = (pad_end[-1] // MOE_BLOCK).astype(jnp.int32)
    blk_ids = jnp.arange(n_blk, dtype=jnp.int32)
    blk_expert = jnp.minimum(jnp.searchsorted(pad_end, blk_ids * MOE_BLOCK, side="right"),
                             N_EXPERTS - 1).astype(jnp.int32)
    blk_expert = jnp.where(blk_ids < n_active, blk_expert, blk_expert[n_active - 1])
    e_pos = jnp.repeat(blk_expert, MOE_BLOCK)
    rank = jnp.arange(length, dtype=jnp.int32) - pad_start[e_pos]
    valid = rank < counts[e_pos]
    slot = jnp.where(valid, order[jnp.clip(start[e_pos] + rank, 0, n_slot - 1)], 0)
    buf_tok = slot // TOP_K
    buf_w = jnp.where(valid, flat_w[slot], 0.0)
    dest = pad_start[s_e] + (jnp.arange(n_slot, dtype=jnp.int32) - start[s_e])
    _, pos = lax.sort((order, dest.astype(jnp.int32)), num_keys=1)
    return blk_expert, n_active.reshape(1), buf_tok, buf_w.reshape(length, 1), pos


def _rotate_half_cols(w):
    half = w.shape[-1] // 2
    return jnp.concatenate([-w[..., half:], w[..., :half]], axis=-1)


def _layer(x, mem, positions, g_mix, w_in, w_conv, g_q_a, w_q_up, g_kv_a, w_kv_up, g_conv_out,
           g_attn_out, w_out, g_xattn, g_mem, w_xq, w_xk, w_xv, w_xo, g_moe, w_group, b_group,
           w_router, b_router, w_gate, w_up, w_down):
    batch, seq, d = x.shape
    mem_len = mem.shape[1]
    n = batch * seq
    cw = w_conv.shape[1]
    ql = g_q_a.shape[0]
    kvl = g_kv_a.shape[0]
    c_rope = 3 * cw + ql + kvl
    row2 = lambda g: g.reshape(1, -1)

    w_kr = w_in[:, c_rope:c_rope + QK_ROPE]
    w_in_ext = jnp.concatenate([w_in[:, :c_rope + QK_ROPE], _rotate_half_cols(w_kr)], axis=1).astype(BF16)
    inv_freq = 1.0 / (ROPE_THETA ** (jnp.arange(0, QK_ROPE, 2, dtype=F32) / QK_ROPE))
    freq = jnp.tile(inv_freq, LANES // inv_freq.shape[0]).reshape(1, LANES)

    ycn, cqn, ckvn, kr, cs = _in_proj(
        x.reshape(n, d), positions.reshape(n, 1), row2(g_mix), w_in_ext, w_conv, row2(g_q_a),
        row2(g_kv_a), row2(g_conv_out), freq, seq=seq, cw=cw, ql=ql, kvl=kvl)

    wq = w_q_up.reshape(ql, N_MLA_HEADS, QK_NOPE + QK_ROPE).transpose(1, 0, 2)
    wq_nope, wq_rope = wq[..., :QK_NOPE], wq[..., QK_NOPE:]
    z64 = jnp.zeros((N_MLA_HEADS, ql, QK_ROPE), F32)
    wqa = jnp.concatenate([wq_nope, wq_rope, z64], axis=-1).astype(BF16)
    wqb = jnp.concatenate([jnp.zeros_like(wq_nope), _rotate_half_cols(wq_rope), z64], axis=-1).astype(BF16)
    wkv = w_kv_up.reshape(kvl, N_MLA_HEADS, QK_NOPE + V_HEAD).transpose(1, 0, 2).astype(BF16)
    wk = wkv[..., :QK_NOPE]
    wvt = wkv[..., QK_NOPE:].transpose(0, 2, 1)

    yan = _attention(cqn, cs, ckvn, kr, wqa, wqb, wk, wvt, row2(g_attn_out), batch=batch, seq=seq)

    w_xkv = jnp.concatenate([w_xk, w_xv], axis=1).astype(BF16)
    kmem, vmem = _mem_kv(mem.reshape(batch * mem_len, d), row2(g_mem), w_xkv, mem_len=mem_len)

    pad = LANES - N_GROUPS - N_EXPERTS
    w_rt = jnp.concatenate([w_group, w_router, jnp.zeros((d, pad), F32)], axis=1)
    w_rt_hi = w_rt.astype(BF16)
    w_rt_lo = (w_rt - w_rt_hi.astype(F32)).astype(BF16)
    b_rt = jnp.concatenate([b_group, b_router, jnp.zeros((pad,), F32)]).reshape(1, LANES)
    x2, xp, route = _mid(x.reshape(n, d), ycn, yan, w_out.astype(BF16), row2(g_xattn), w_xq.astype(BF16),
                         kmem, vmem, w_xo.astype(BF16), row2(g_moe), w_rt_hi, w_rt_lo, b_rt,
                         seq=seq, mem_len=mem_len)

    expert_id = route[:, :TOP_K].astype(jnp.int32)
    gate_w = route[:, TOP_K:2 * TOP_K]
    blk_expert, n_active, buf_tok, buf_w, pos = _dispatch(expert_id, gate_w)
    y = _experts(blk_expert, n_active, buf_tok, xp, buf_w, w_gate, w_up, w_down)
    pos_tiles = pos.reshape(n // ROW_TILE, ROW_TILE, TOP_K).transpose(0, 2, 1).reshape(-1)
    return x2, pos_tiles, y


def kernel(x, mem, positions, g_mix, w_in, w_conv, g_q_a, w_q_up, g_kv_a, w_kv_up, g_conv_out, g_attn_out, w_out, g_xattn, g_mem, w_xq, w_xk, w_xv, w_xo, g_moe, w_group, b_group, w_router, b_router, w_gate, w_up, w_down, g_final):
    depth = g_mix.shape[0]
    assert depth == 1, "the combine stage applies the final norm, so exactly one layer is supported"
    shape = x.shape
    x2, pos_tiles, y = _layer(
        x, mem, positions, g_mix[0], w_in[0], w_conv[0], g_q_a[0], w_q_up[0], g_kv_a[0], w_kv_up[0],
        g_conv_out[0], g_attn_out[0], w_out[0], g_xattn[0], g_mem[0], w_xq[0], w_xk[0], w_xv[0],
        w_xo[0], g_moe[0], w_group[0], b_group[0], w_router[0], b_router[0], w_gate[0], w_up[0], w_down[0])
    out = _combine(pos_tiles, x2, g_final.reshape(1, -1), y)
    return out.reshape(shape)
```

```python
import functools
import math

import jax
import jax.numpy as jnp
from jax import lax
from jax.experimental import pallas as pl
from jax.experimental.pallas import tpu as pltpu

F32 = jnp.float32
BF16 = jnp.bfloat16

RMS_EPS = 1e-6
N_MLA_HEADS = 8
QK_NOPE = 128
QK_ROPE = 64
V_HEAD = 128
ROPE_THETA = 10000.0
MASK_VALUE = -1e30
N_X_HEADS = 4
N_GROUPS = 8
EXPERTS_PER_GROUP = 8
N_EXPERTS = N_GROUPS * EXPERTS_PER_GROUP
TOP_K = 2
MOE_BLOCK = 256

LANES = 128
VMEM_LIMIT_BYTES = 60000 * 1024

IN_PROJ_TILE = 512
ROW_TILE = 256
ATTN_TILE = 512
ATTN_HEAD_GROUP = 2


def _rms(x, g):
    ms = jnp.mean(x * x, axis=-1, keepdims=True)
    return x * lax.rsqrt(ms + RMS_EPS) * g


def _resident(shape):
    nd = len(shape)
    return pl.BlockSpec(shape, lambda *_: (0,) * nd, pipeline_mode=pl.Buffered(1))


def _params(n_axes):
    return pltpu.CompilerParams(dimension_semantics=("arbitrary",) * n_axes,
                                vmem_limit_bytes=VMEM_LIMIT_BYTES)


def _in_proj_kernel(x_ref, pos_ref, gmix_ref, w_ref, wconv_ref, gq_ref, gkv_ref, gco_ref, freq_ref,
                    ycn_ref, cqn_ref, ckvn_ref, kr_ref, cs_ref, carry_ref,
                    *, tiles_per_seq, cw, ql, kvl):
    i = pl.program_id(0)
    tm = x_ref.shape[0]
    xn = _rms(x_ref[...], gmix_ref[...]).astype(BF16)

    def mm(lo, hi):
        return jnp.dot(xn, w_ref[:, lo:hi], preferred_element_type=F32)

    zb, zc, zh = mm(0, cw), mm(cw, 2 * cw), mm(2 * cw, 3 * cw)
    u = zc * zh

    @pl.when(i % tiles_per_seq == 0)
    def _():
        carry_ref[...] = jnp.zeros_like(carry_ref)

    prev = carry_ref[...]
    carry_ref[...] = u[tm - 8:, :]
    row = lax.broadcasted_iota(jnp.int32, (tm, 1), 0)
    u1 = jnp.where(row == 0, prev[7:8, :], pltpu.roll(u, 1, axis=0))
    u2 = jnp.where(row == 0, prev[6:7, :],
                   jnp.where(row == 1, prev[7:8, :], pltpu.roll(u, 2, axis=0)))
    wc = wconv_ref[...]
    conv = wc[0:1, :] * u2 + wc[1:2, :] * u1 + wc[2:3, :] * u
    ycn_ref[...] = _rms(zb * conv, gco_ref[...]).astype(BF16)

    c3 = 3 * cw
    cqn_ref[...] = _rms(mm(c3, c3 + ql), gq_ref[...]).astype(BF16)
    ckvn_ref[...] = _rms(mm(c3 + ql, c3 + ql + kvl), gkv_ref[...]).astype(BF16)

    zk = mm(c3 + ql + kvl, c3 + ql + kvl + LANES)
    ang = pos_ref[...].astype(F32) * freq_ref[...]
    cos, sin = jnp.cos(ang), jnp.sin(ang)
    lane = lax.broadcasted_iota(jnp.int32, (tm, LANES), 1)
    lo = lane < QK_ROPE
    cos_sin = jnp.where(lo, cos, sin)
    t = zk * cos_sin
    kr = t + pltpu.roll(t, QK_ROPE, axis=1)
    kr_ref[...] = jnp.where(lo, kr, 0.0).astype(BF16)
    cs_ref[...] = jnp.concatenate([cos_sin, jnp.where(lo, sin, cos)], axis=-1)


def _in_proj(x2d, pos2d, g_mix, w_in_ext, w_conv, g_q_a, g_kv_a, g_conv_out, freq, *, seq, cw, ql, kvl):
    n, d = x2d.shape
    tm = IN_PROJ_TILE
    wcols = w_in_ext.shape[1]
    kern = functools.partial(_in_proj_kernel, tiles_per_seq=seq // tm, cw=cw, ql=ql, kvl=kvl)
    row = lambda c: pl.BlockSpec((tm, c), lambda i: (i, 0))
    return pl.pallas_call(
        kern,
        grid=(n // tm,),
        in_specs=[row(d), row(1), _resident((1, d)), _resident((d, wcols)), _resident((3, cw)),
                  _resident((1, ql)), _resident((1, kvl)), _resident((1, cw)), _resident((1, LANES))],
        out_specs=[row(cw), row(ql), row(kvl), row(LANES), row(2 * LANES)],
        out_shape=[jax.ShapeDtypeStruct((n, cw), BF16), jax.ShapeDtypeStruct((n, ql), BF16),
                   jax.ShapeDtypeStruct((n, kvl), BF16), jax.ShapeDtypeStruct((n, LANES), BF16),
                   jax.ShapeDtypeStruct((n, 2 * LANES), F32)],
        scratch_shapes=[pltpu.VMEM((8, cw), F32)],
        compiler_params=_params(1),
        name="in_proj",
    )(x2d, pos2d, g_mix, w_in_ext, w_conv, g_q_a, g_kv_a, g_conv_out, freq)


def _attn_kernel(cqn_ref, cs_ref, ckvn_ref, kr_ref, wqa_ref, wqb_ref, wk_ref, wvt_ref, gao_ref,
                 out_ref, k_scr, vt_scr, o_scr, *, scale):
    qi = pl.program_id(1)
    hg = pl.program_id(2)
    n_groups = pl.num_programs(2)
    group = wqa_ref.shape[0]
    tq = cqn_ref.shape[0]
    tk = tq
    nt = (((1,), (1,)), ((), ()))

    @pl.when(qi == 0)
    def _():
        ckv = ckvn_ref[...]
        for j in range(group):
            h = hg * group + j
            k_scr[h, :, 0:QK_NOPE] = jnp.dot(ckv, wk_ref[j], preferred_element_type=F32).astype(BF16)
            k_scr[h, :, QK_NOPE:2 * QK_NOPE] = kr_ref[...]
            vt = lax.dot_general(wvt_ref[j], ckv, nt, preferred_element_type=F32)
            for kb in range(vt_scr.shape[1]):
                vt_scr[h, kb] = vt[:, kb * tk:(kb + 1) * tk].astype(BF16)

    cq = cqn_ref[...]
    cs = cs_ref[...]
    ones = jnp.ones((tq, LANES), F32)
    m1 = jnp.concatenate([ones, cs[:, :LANES]], axis=-1) * scale
    m2 = jnp.concatenate([ones, cs[:, LANES:]], axis=-1) * scale
    qs = [(jnp.dot(cq, wqa_ref[j], preferred_element_type=F32) * m1
           + jnp.dot(cq, wqb_ref[j], preferred_element_type=F32) * m2).astype(BF16) for j in range(group)]

    def head_output(j, n_blocks):
        h = hg * group + j
        m = l = acc = None
        for kb in range(n_blocks):
            k = k_scr[h, kb * tk:(kb + 1) * tk, :]
            s = lax.dot_general(k, qs[j], nt, preferred_element_type=F32)
            if kb == n_blocks - 1:
                kpos = lax.broadcasted_iota(jnp.int32, (tk, tq), 0)
                qpos = lax.broadcasted_iota(jnp.int32, (tk, tq), 1)
                s = jnp.where(kpos <= qpos, s, MASK_VALUE)
            blk_max = jnp.max(s, axis=0, keepdims=True)
            m_new = blk_max if m is None else jnp.maximum(m, blk_max)
            p = jnp.exp(s - m_new)
            psum = jnp.sum(p, axis=0, keepdims=True)
            pv = jnp.dot(vt_scr[h, kb], p.astype(BF16), preferred_element_type=F32)
            if m is None:
                l, acc = psum, pv
            else:
                alpha = jnp.exp(m - m_new)
                l, acc = alpha * l + psum, alpha * acc + pv
            m = m_new
        return (acc / l).T

    for qv in range(vt_scr.shape[1]):
        @pl.when(qi == qv)
        def _(qv=qv):
            for j in range(group):
                o_scr[hg * group + j] = head_output(j, qv + 1)

    @pl.when(hg == n_groups - 1)
    def _():
        o = jnp.concatenate([o_scr[j] for j in range(N_MLA_HEADS)], axis=-1)
        out_ref[...] = _rms(o, gao_ref[...]).astype(BF16)


def _attention(cqn, cs, ckvn, kr, wqa, wqb, wk, wvt, g_attn_out, *, batch, seq):
    n, ql = cqn.shape
    kvl = ckvn.shape[1]
    tq = ATTN_TILE
    nq = seq // tq
    aw = N_MLA_HEADS * V_HEAD
    kern = functools.partial(_attn_kernel, scale=1.0 / math.sqrt(QK_NOPE + QK_ROPE))
    qrow = lambda c: pl.BlockSpec((tq, c), lambda b, qi, h: (b * nq + qi, 0))
    brow = lambda c: pl.BlockSpec((seq, c), lambda b, qi, h: (b, 0))
    group = ATTN_HEAD_GROUP
    head = lambda r, c: pl.BlockSpec((group, r, c), lambda b, qi, h: (h, 0, 0))
    return pl.pallas_call(
        kern,
        grid=(batch, nq, N_MLA_HEADS // group),
        in_specs=[qrow(ql), qrow(2 * LANES), brow(kvl), brow(LANES),
                  head(ql, 2 * QK_NOPE), head(ql, 2 * QK_NOPE), head(kvl, QK_NOPE), head(V_HEAD, kvl),
                  pl.BlockSpec((1, aw), lambda b, qi, h: (0, 0))],
        out_specs=qrow(aw),
        out_shape=jax.ShapeDtypeStruct((n, aw), BF16),
        scratch_shapes=[pltpu.VMEM((N_MLA_HEADS, seq, 2 * QK_NOPE), BF16),
                        pltpu.VMEM((N_MLA_HEADS, nq, V_HEAD, tq), BF16),
                        pltpu.VMEM((N_MLA_HEADS, tq, V_HEAD), F32)],
        compiler_params=_params(3),
        name="mla_attention",
    )(cqn, cs, ckvn, kr, wqa, wqb, wk, wvt, g_attn_out)


def _mem_kv_kernel(mem_ref, g_ref, w_ref, k_ref, v_ref):
    d = mem_ref.shape[1]
    mn = _rms(mem_ref[...], g_ref[...]).astype(BF16)
    k_ref[...] = jnp.dot(mn, w_ref[:, :d], preferred_element_type=F32).astype(BF16)
    v_ref[...] = jnp.dot(mn, w_ref[:, d:], preferred_element_type=F32).astype(BF16)


def _mem_kv(mem2d, g_mem, w_xkv, *, mem_len):
    n, d = mem2d.shape
    row = pl.BlockSpec((mem_len, d), lambda i: (i, 0))
    return pl.pallas_call(
        _mem_kv_kernel,
        grid=(n // mem_len,),
        in_specs=[row, _resident((1, d)), _resident((d, 2 * d))],
        out_specs=[row, row],
        out_shape=[jax.ShapeDtypeStruct((n, d), BF16)] * 2,
        compiler_params=_params(1),
        name="mem_kv",
    )(mem2d, g_mem, w_xkv)


def _route(logits):
    tm = logits.shape[0]
    lane = lax.broadcasted_iota(jnp.int32, (tm, LANES), 1)
    big = jnp.int32(1 << 20)
    ninf = -jnp.inf
    gl = jnp.where(lane < N_GROUPS, logits, ninf)
    gm = jnp.max(gl, axis=-1, keepdims=True)
    gidx = jnp.min(jnp.where(gl == gm, lane, big), axis=-1, keepdims=True)
    g_p = 1.0 / jnp.sum(jnp.exp(gl - gm), axis=-1, keepdims=True)
    first = N_GROUPS + gidx * EXPERTS_PER_GROUP
    el = jnp.where(lane >= first, jnp.where(lane < first + EXPERTS_PER_GROUP, logits, ninf), ninf)
    top1 = jnp.max(el, axis=-1, keepdims=True)
    i1 = jnp.min(jnp.where(el == top1, lane, big), axis=-1, keepdims=True)
    el2 = jnp.where(lane == i1, ninf, el)
    top2 = jnp.max(el2, axis=-1, keepdims=True)
    i2 = jnp.min(jnp.where(el2 == top2, lane, big), axis=-1, keepdims=True)
    r = jnp.exp(top2 - top1)
    w1 = g_p / (1.0 + r)
    w2 = g_p * r / (1.0 + r)
    e1 = (i1 - N_GROUPS).astype(F32)
    e2 = (i2 - N_GROUPS).astype(F32)
    return jnp.where(lane == 0, e1, jnp.where(lane == 1, e2,
                     jnp.where(lane == 2, w1, jnp.where(lane == 3, w2, 0.0))))


def _mid_kernel(x_ref, ycn_ref, yan_ref, wout_ref, gx_ref, wxq_ref, km_ref, vm_ref, wxo_ref,
                gmoe_ref, wrh_ref, wrl_ref, brt_ref, x2_ref, xp_ref, route_ref):
    d = x_ref.shape[1]
    cw = ycn_ref.shape[1]
    xd = d // N_X_HEADS
    x1 = (x_ref[...]
          + jnp.dot(ycn_ref[...], wout_ref[:cw, :], preferred_element_type=F32)
          + jnp.dot(yan_ref[...], wout_ref[cw:, :], preferred_element_type=F32))
    h2 = _rms(x1, gx_ref[...]).astype(BF16)
    q = (jnp.dot(h2, wxq_ref[...], preferred_element_type=F32) * (1.0 / math.sqrt(xd))).astype(BF16)
    heads = []
    for hh in range(N_X_HEADS):
        sl = slice(hh * xd, (hh + 1) * xd)
        s = lax.dot_general(q[:, sl], km_ref[:, sl], (((1,), (1,)), ((), ())),
                            preferred_element_type=F32)
        e = jnp.exp(s - jnp.max(s, axis=-1, keepdims=True))
        p = (e / jnp.sum(e, axis=-1, keepdims=True)).astype(BF16)
        heads.append(jnp.dot(p, vm_ref[:, sl], preferred_element_type=F32).astype(BF16))
    o = jnp.concatenate(heads, axis=-1)
    x2 = x1 + jnp.dot(o, wxo_ref[...], preferred_element_type=F32)
    x2_ref[...] = x2
    xn = _rms(x2, gmoe_ref[...])
    xn_hi = xn.astype(BF16)
    xn_lo = (xn - xn_hi.astype(F32)).astype(BF16)
    logits = (jnp.dot(xn_hi, wrh_ref[...], preferred_element_type=F32)
              + (jnp.dot(xn_lo, wrh_ref[...], preferred_element_type=F32)
                 + jnp.dot(xn_hi, wrl_ref[...], preferred_element_type=F32))) + brt_ref[...]
    route_ref[...] = _route(logits)
    _store_rows(xp_ref, xn)


def _mid(x2d, ycn, yan, w_out, g_xattn, w_xq, kmem, vmem, w_xo, g_moe, w_rt_hi, w_rt_lo, b_rt,
         *, seq, mem_len):
    n, d = x2d.shape
    cw = ycn.shape[1]
    tm = ROW_TILE
    sub = _row_sublanes(d)
    per_seq = seq // tm
    row = lambda c: pl.BlockSpec((tm, c), lambda i: (i, 0))
    memblk = pl.BlockSpec((mem_len, d), lambda i: (i // per_seq, 0))
    return pl.pallas_call(
        _mid_kernel,
        grid=(n // tm,),
        in_specs=[row(d), row(cw), row(yan.shape[1]), _resident((d, d)), _resident((1, d)),
                  _resident((d, d)), memblk, memblk, _resident((d, d)), _resident((1, d)),
                  _resident((d, LANES)), _resident((d, LANES)), _resident((1, LANES))],
        out_specs=[row(d), pl.BlockSpec((tm * sub, LANES), lambda i: (i, 0)), row(LANES)],
        out_shape=[jax.ShapeDtypeStruct((n, d), F32), jax.ShapeDtypeStruct((n * sub, LANES), F32),
                   jax.ShapeDtypeStruct((n, LANES), F32)],
        compiler_params=_params(1),
        name="mid",
    )(x2d, ycn, yan, w_out, g_xattn, w_xq, kmem, vmem, w_xo, g_moe, w_rt_hi, w_rt_lo, b_rt)


DMA_PRIORITIES = 2
GATHER_SLOTS = 3
TILE_SUBLANES = 8
GATHER_PITCH_PAD = 4


def _gather_pitch(sub):
    return sub + GATHER_PITCH_PAD


def _row_sublanes(d):
    sub = d // LANES
    assert sub % TILE_SUBLANES == 0, "a row must cover whole (8, 128) tiles"
    return sub


def _store_rows(dst_ref, v):
    rows, d = v.shape
    sub = _row_sublanes(d)
    for s in range(sub):
        dst_ref[pl.ds(s, rows, stride=sub), :] = v[:, s * LANES:(s + 1) * LANES]


def _load_gathered_rows(buf_ref, first_row, rows, sub):
    pitch = _gather_pitch(sub)
    return jnp.concatenate([buf_ref[pl.ds(first_row * pitch + s, rows, stride=pitch), :]
                            for s in range(sub)], axis=-1)


def _gather_rows(idx_ref, base, n_rows, src_hbm, dst_buf, sem, sub, inline=False, idx_shift=0):
    pitch = _gather_pitch(sub)

    def start_row(r, priority):
        t = idx_ref[base + r] >> idx_shift
        pltpu.make_async_copy(src_hbm.at[pl.ds(pl.multiple_of(t * sub, sub), sub)],
                              dst_buf.at[pl.ds(r * pitch, sub)], sem).start(priority=priority)

    if inline:
        for r in range(n_rows):
            start_row(r, r % DMA_PRIORITIES)
    else:
        def body(r2, carry):
            for p in range(DMA_PRIORITIES):
                start_row(r2 * DMA_PRIORITIES + p, p)
            return carry
        lax.fori_loop(0, n_rows // DMA_PRIORITIES, body, 0, unroll=4)


def _wait_rows(dst_buf, n_rows, sub, sem):
    span = dst_buf.at[pl.ds(0, n_rows * sub)]
    pltpu.make_async_copy(span, span, sem).wait()


def _experts_kernel(be_ref, na_ref, src_ref, order_ref, xp_hbm, wg_ref, wu_ref, wd_ref, y_ref,
                    xbuf, sem, wgb, wub, wdb):
    i = pl.program_id(0)
    na = na_ref[0]
    n_slots = xbuf.shape[0]
    slot = i % n_slots
    ahead = n_slots - 1
    sub = _row_sublanes(wgb.shape[0])
    blk = xbuf.shape[1] // _gather_pitch(sub)
    tok_shift = TOP_K.bit_length() - 1

    def gather(b, dst_slot, inline):
        _gather_rows(order_ref, src_ref[b], blk, xp_hbm, xbuf.at[dst_slot], sem.at[dst_slot], sub,
                     inline=inline, idx_shift=tok_shift)

    @pl.when(i == 0)
    def _():
        for b in range(ahead):
            gather(b, b, False)

    @pl.when(i < na + ahead)
    def _():
        _wait_rows(xbuf.at[slot], blk, sub, sem.at[slot])

    @pl.when(i < na)
    def _():
        new_expert = jnp.logical_or(i == 0, be_ref[i] != be_ref[jnp.maximum(i - 1, 0)])

        @pl.when(new_expert)
        def _():
            wgb[...] = wg_ref[...].astype(BF16)
            wub[...] = wu_ref[...].astype(BF16)
            wdb[...] = wd_ref[...].astype(BF16)

        xb = _load_gathered_rows(xbuf.at[slot], 0, blk, sub).astype(BF16)
        gather(i + ahead, (i + ahead) % n_slots, True)
        g = jnp.dot(xb, wgb[...], preferred_element_type=F32)
        u = jnp.dot(xb, wub[...], preferred_element_type=F32)
        act = (g * jax.nn.sigmoid(g) * u).astype(BF16)
        _store_rows(y_ref, jnp.dot(act, wdb[...], preferred_element_type=F32))

    @pl.when(i >= na)
    def _():
        y_ref[...] = jnp.zeros_like(y_ref)


def _experts(blk_expert, n_active, blk_src, order_pad, xp, w_gate, w_up, w_down):
    n_blk = blk_expert.shape[0]
    d = w_gate.shape[1]
    de = w_gate.shape[2]
    tile = (MOE_BLOCK * _row_sublanes(d), LANES)
    wspec = lambda r, c: pl.BlockSpec((None, r, c), lambda i, be, na, src, order: (be[i], 0, 0))
    grid_spec = pltpu.PrefetchScalarGridSpec(
        num_scalar_prefetch=4,
        grid=(n_blk,),
        in_specs=[pl.BlockSpec(memory_space=pl.ANY), wspec(d, de), wspec(d, de), wspec(de, d)],
        out_specs=pl.BlockSpec(tile, lambda i, be, na, src, order: (i, 0)),
        scratch_shapes=[pltpu.VMEM((GATHER_SLOTS, MOE_BLOCK * _gather_pitch(_row_sublanes(d)), LANES), F32),
                        pltpu.SemaphoreType.DMA((GATHER_SLOTS,)),
                        pltpu.VMEM((d, de), BF16), pltpu.VMEM((d, de), BF16), pltpu.VMEM((de, d), BF16)],
    )
    return pl.pallas_call(
        _experts_kernel,
        grid_spec=grid_spec,
        out_shape=jax.ShapeDtypeStruct((n_blk * tile[0], LANES), F32),
        compiler_params=_params(1),
        name="experts",
    )(blk_expert, n_active, blk_src, order_pad, xp, w_gate, w_up, w_down)


def _combine_kernel(pos_ref, x2_ref, route_ref, g_ref, y_hbm, out_ref, ybuf, sem):
    i = pl.program_id(0)
    n = pl.num_programs(0)
    slot = i % 2
    tm = x2_ref.shape[0]
    rows = TOP_K * tm
    sub = _row_sublanes(x2_ref.shape[1])

    @pl.when(i == 0)
    def _():
        _gather_rows(pos_ref, 0, rows, y_hbm, ybuf.at[0], sem.at[0], sub)

    _wait_rows(ybuf.at[slot], rows, sub, sem.at[slot])
    ys = [_load_gathered_rows(ybuf.at[slot], k * tm, tm, sub) for k in range(TOP_K)]
    nxt = jnp.minimum(i + 1, n - 1)
    _gather_rows(pos_ref, nxt * rows, rows, y_hbm, ybuf.at[1 - slot], sem.at[1 - slot], sub, inline=True)
    route = route_ref[...]
    moe = None
    for k in range(TOP_K):
        term = ys[k] * route[:, TOP_K + k:TOP_K + k + 1]
        moe = term if moe is None else moe + term
    out_ref[...] = _rms(x2_ref[...] + moe, g_ref[...])

    @pl.when(i == n - 1)
    def _():
        _wait_rows(ybuf.at[1 - slot], rows, sub, sem.at[1 - slot])


def _combine(pos, x2, route, g_final, y):
    n, d = x2.shape
    tm = ROW_TILE
    grid_spec = pltpu.PrefetchScalarGridSpec(
        num_scalar_prefetch=1,
        grid=(n // tm,),
        in_specs=[pl.BlockSpec((tm, d), lambda i, pos: (i, 0)),
                  pl.BlockSpec((tm, LANES), lambda i, pos: (i, 0)),
                  pl.BlockSpec((1, d), lambda i, pos: (0, 0)),
                  pl.BlockSpec(memory_space=pl.ANY)],
        out_specs=pl.BlockSpec((tm, d), lambda i, pos: (i, 0)),
        scratch_shapes=[pltpu.VMEM((2, TOP_K * tm * _gather_pitch(_row_sublanes(d)), LANES), F32),
                        pltpu.SemaphoreType.DMA((2,))],
    )
    return pl.pallas_call(
        _combine_kernel,
        grid_spec=grid_spec,
        out_shape=jax.ShapeDtypeStruct((n, d), F32),
        compiler_params=_params(1),
        name="combine",
    )(pos, x2, route, g_final, y)


def _dispatch(expert_id):
    n_tok = expert_id.shape[0]
    n_slot = n_tok * TOP_K
    flat_e = expert_id.reshape(-1)
    iota_slot = jnp.arange(n_slot, dtype=jnp.int32)
    s_e, order = lax.sort((flat_e, iota_slot), num_keys=1)
    experts = jnp.arange(N_EXPERTS, dtype=jnp.int32)
    counts = jnp.sum((flat_e[:, None] == experts[None, :]).astype(jnp.int32), axis=0)
    start = jnp.cumsum(counts) - counts
    padded = ((counts + MOE_BLOCK - 1) // MOE_BLOCK) * MOE_BLOCK
    pad_end = jnp.cumsum(padded)
    pad_start = pad_end - padded
    n_blk = n_slot // MOE_BLOCK + N_EXPERTS + GATHER_SLOTS - 2
    n_active = (pad_end[-1] // MOE_BLOCK).astype(jnp.int32)
    blk_ids = jnp.arange(n_blk, dtype=jnp.int32)
    blk_first = blk_ids * MOE_BLOCK
    blk_expert = jnp.minimum(jnp.sum((pad_end[None, :] <= blk_first[:, None]).astype(jnp.int32), axis=1),
                             N_EXPERTS - 1)
    blk_expert = jnp.where(blk_ids < n_active, blk_expert, blk_expert[n_active - 1])
    shift = pad_start - start
    onehot = (blk_expert[:, None] == experts[None, :]).astype(jnp.int32)
    blk_src = jnp.clip(blk_first - jnp.sum(onehot * shift[None, :], axis=1), 0, n_slot)
    order_pad = jnp.concatenate([order, jnp.zeros((MOE_BLOCK,), jnp.int32)])
    shift_sorted = jnp.sum((s_e[:, None] == experts[None, :]).astype(jnp.int32) * shift[None, :], axis=1)
    _, pos = lax.sort((order, iota_slot + shift_sorted), num_keys=1)
    return blk_expert, n_active.reshape(1), blk_src, order_pad, pos


def _rotate_half_cols(w):
    half = w.shape[-1] // 2
    return jnp.concatenate([-w[..., half:], w[..., :half]], axis=-1)


def _layer(x, mem, positions, g_mix, w_in, w_conv, g_q_a, w_q_up, g_kv_a, w_kv_up, g_conv_out,
           g_attn_out, w_out, g_xattn, g_mem, w_xq, w_xk, w_xv, w_xo, g_moe, w_group, b_group,
           w_router, b_router, w_gate, w_up, w_down):
    batch, seq, d = x.shape
    mem_len = mem.shape[1]
    n = batch * seq
    cw = w_conv.shape[1]
    ql = g_q_a.shape[0]
    kvl = g_kv_a.shape[0]
    c_rope = 3 * cw + ql + kvl
    row2 = lambda g: g.reshape(1, -1)

    w_kr = w_in[:, c_rope:c_rope + QK_ROPE]
    w_in_ext = jnp.concatenate([w_in[:, :c_rope + QK_ROPE], _rotate_half_cols(w_kr)], axis=1).astype(BF16)
    inv_freq = 1.0 / (ROPE_THETA ** (jnp.arange(0, QK_ROPE, 2, dtype=F32) / QK_ROPE))
    freq = jnp.tile(inv_freq, LANES // inv_freq.shape[0]).reshape(1, LANES)

    ycn, cqn, ckvn, kr, cs = _in_proj(
        x.reshape(n, d), positions.reshape(n, 1), row2(g_mix), w_in_ext, w_conv, row2(g_q_a),
        row2(g_kv_a), row2(g_conv_out), freq, seq=seq, cw=cw, ql=ql, kvl=kvl)

    wq = w_q_up.reshape(ql, N_MLA_HEADS, QK_NOPE + QK_ROPE).transpose(1, 0, 2)
    wq_nope, wq_rope = wq[..., :QK_NOPE], wq[..., QK_NOPE:]
    z64 = jnp.zeros((N_MLA_HEADS, ql, QK_ROPE), F32)
    wqa = jnp.concatenate([wq_nope, wq_rope, z64], axis=-1).astype(BF16)
    wqb = jnp.concatenate([jnp.zeros_like(wq_nope), _rotate_half_cols(wq_rope), z64], axis=-1).astype(BF16)
    wkv = w_kv_up.reshape(kvl, N_MLA_HEADS, QK_NOPE + V_HEAD).transpose(1, 0, 2).astype(BF16)
    wk = wkv[..., :QK_NOPE]
    wvt = wkv[..., QK_NOPE:].transpose(0, 2, 1)

    yan = _attention(cqn, cs, ckvn, kr, wqa, wqb, wk, wvt, row2(g_attn_out), batch=batch, seq=seq)

    w_xkv = jnp.concatenate([w_xk, w_xv], axis=1).astype(BF16)
    kmem, vmem = _mem_kv(mem.reshape(batch * mem_len, d), row2(g_mem), w_xkv, mem_len=mem_len)

    pad = LANES - N_GROUPS - N_EXPERTS
    w_rt = jnp.concatenate([w_group, w_router, jnp.zeros((d, pad), F32)], axis=1)
    w_rt_hi = w_rt.astype(BF16)
    w_rt_lo = (w_rt - w_rt_hi.astype(F32)).astype(BF16)
    b_rt = jnp.concatenate([b_group, b_router, jnp.zeros((pad,), F32)]).reshape(1, LANES)
    x2, xp, route = _mid(x.reshape(n, d), ycn, yan, w_out.astype(BF16), row2(g_xattn), w_xq.astype(BF16),
                         kmem, vmem, w_xo.astype(BF16), row2(g_moe), w_rt_hi, w_rt_lo, b_rt,
                         seq=seq, mem_len=mem_len)

    expert_id = route[:, :TOP_K].astype(jnp.int32)
    blk_expert, n_active, blk_src, order_pad, pos = _dispatch(expert_id)
    y = _experts(blk_expert, n_active, blk_src, order_pad, xp, w_gate, w_up, w_down)
    pos_tiles = pos.reshape(n // ROW_TILE, ROW_TILE, TOP_K).transpose(0, 2, 1).reshape(-1)
    return x2, route, pos_tiles, y


def kernel(x, mem, positions, g_mix, w_in, w_conv, g_q_a, w_q_up, g_kv_a, w_kv_up, g_conv_out, g_attn_out, w_out, g_xattn, g_mem, w_xq, w_xk, w_xv, w_xo, g_moe, w_group, b_group, w_router, b_router, w_gate, w_up, w_down, g_final):
    depth = g_mix.shape[0]
    assert depth == 1, "the combine stage applies the final norm, so exactly one layer is supported"
    shape = x.shape
    x2, route, pos_tiles, y = _layer(
        x, mem, positions, g_mix[0], w_in[0], w_conv[0], g_q_a[0], w_q_up[0], g_kv_a[0], w_kv_up[0],
        g_conv_out[0], g_attn_out[0], w_out[0], g_xattn[0], g_mem[0], w_xq[0], w_xk[0], w_xv[0],
        w_xo[0], g_moe[0], w_group[0], b_group[0], w_router[0], b_router[0], w_gate[0], w_up[0], w_down[0])
    out = _combine(pos_tiles, x2, route, g_final.reshape(1, -1), y)
    return out.reshape(shape)
```

```python
import functools
import math

import jax
import jax.numpy as jnp
from jax import lax
from jax.experimental import pallas as pl
from jax.experimental.pallas import tpu as pltpu

F32 = jnp.float32
BF16 = jnp.bfloat16

RMS_EPS = 1e-6
N_MLA_HEADS = 8
QK_NOPE = 128
QK_ROPE = 64
V_HEAD = 128
ROPE_THETA = 10000.0
MASK_VALUE = -1e30
N_X_HEADS = 4
N_GROUPS = 8
EXPERTS_PER_GROUP = 8
N_EXPERTS = N_GROUPS * EXPERTS_PER_GROUP
TOP_K = 2
MOE_BLOCK = 256

LANES = 128
VMEM_LIMIT_BYTES = 60000 * 1024

IN_PROJ_TILE = 512
ROW_TILE = 256
ATTN_TILE = 512
ATTN_HEAD_GROUP = 2


def _rms(x, g):
    ms = jnp.mean(x * x, axis=-1, keepdims=True)
    return x * lax.rsqrt(ms + RMS_EPS) * g


def _resident(shape):
    nd = len(shape)
    return pl.BlockSpec(shape, lambda *_: (0,) * nd, pipeline_mode=pl.Buffered(1))


def _params(n_axes):
    return pltpu.CompilerParams(dimension_semantics=("arbitrary",) * n_axes,
                                vmem_limit_bytes=VMEM_LIMIT_BYTES)


def _in_proj_kernel(x_ref, pos_ref, gmix_ref, w_ref, wconv_ref, gq_ref, gkv_ref, gco_ref, freq_ref,
                    ycn_ref, cqn_ref, ckvn_ref, kr_ref, cs_ref, carry_ref,
                    *, tiles_per_seq, cw, ql, kvl):
    i = pl.program_id(0)
    tm = x_ref.shape[0]
    xn = _rms(x_ref[...], gmix_ref[...]).astype(BF16)

    def mm(lo, hi):
        return jnp.dot(xn, w_ref[:, lo:hi], preferred_element_type=F32)

    zb, zc, zh = mm(0, cw), mm(cw, 2 * cw), mm(2 * cw, 3 * cw)
    u = zc * zh

    @pl.when(i % tiles_per_seq == 0)
    def _():
        carry_ref[...] = jnp.zeros_like(carry_ref)

    prev = carry_ref[...]
    carry_ref[...] = u[tm - 8:, :]
    row = lax.broadcasted_iota(jnp.int32, (tm, 1), 0)
    u1 = jnp.where(row == 0, prev[7:8, :], pltpu.roll(u, 1, axis=0))
    u2 = jnp.where(row == 0, prev[6:7, :],
                   jnp.where(row == 1, prev[7:8, :], pltpu.roll(u, 2, axis=0)))
    wc = wconv_ref[...]
    conv = wc[0:1, :] * u2 + wc[1:2, :] * u1 + wc[2:3, :] * u
    ycn_ref[...] = _rms(zb * conv, gco_ref[...]).astype(BF16)

    c3 = 3 * cw
    cqn_ref[...] = _rms(mm(c3, c3 + ql), gq_ref[...]).astype(BF16)
    ckvn_ref[...] = _rms(mm(c3 + ql, c3 + ql + kvl), gkv_ref[...]).astype(BF16)

    zk = mm(c3 + ql + kvl, c3 + ql + kvl + LANES)
    ang = pos_ref[...].astype(F32) * freq_ref[...]
    cos, sin = jnp.cos(ang), jnp.sin(ang)
    lane = lax.broadcasted_iota(jnp.int32, (tm, LANES), 1)
    lo = lane < QK_ROPE
    cos_sin = jnp.where(lo, cos, sin)
    t = zk * cos_sin
    kr = t + pltpu.roll(t, QK_ROPE, axis=1)
    kr_ref[...] = jnp.where(lo, kr, 0.0).astype(BF16)
    cs_ref[...] = jnp.concatenate([cos_sin, jnp.where(lo, sin, cos)], axis=-1)


def _in_proj(x2d, pos2d, g_mix, w_in_ext, w_conv, g_q_a, g_kv_a, g_conv_out, freq, *, seq, cw, ql, kvl):
    n, d = x2d.shape
    tm = IN_PROJ_TILE
    wcols = w_in_ext.shape[1]
    kern = functools.partial(_in_proj_kernel, tiles_per_seq=seq // tm, cw=cw, ql=ql, kvl=kvl)
    row = lambda c: pl.BlockSpec((tm, c), lambda i: (i, 0))
    return pl.pallas_call(
        kern,
        grid=(n // tm,),
        in_specs=[row(d), row(1), _resident((1, d)), _resident((d, wcols)), _resident((3, cw)),
                  _resident((1, ql)), _resident((1, kvl)), _resident((1, cw)), _resident((1, LANES))],
        out_specs=[row(cw), row(ql), row(kvl), row(LANES), row(2 * LANES)],
        out_shape=[jax.ShapeDtypeStruct((n, cw), BF16), jax.ShapeDtypeStruct((n, ql), BF16),
                   jax.ShapeDtypeStruct((n, kvl), BF16), jax.ShapeDtypeStruct((n, LANES), BF16),
                   jax.ShapeDtypeStruct((n, 2 * LANES), F32)],
        scratch_shapes=[pltpu.VMEM((8, cw), F32)],
        compiler_params=_params(1),
        name="in_proj",
    )(x2d, pos2d, g_mix, w_in_ext, w_conv, g_q_a, g_kv_a, g_conv_out, freq)


def _attn_kernel(cqn_ref, cs_ref, ckvn_ref, kr_ref, wqa_ref, wqb_ref, wk_ref, wvt_ref, gao_ref,
                 out_ref, k_scr, vt_scr, o_scr, *, scale):
    qi = pl.program_id(1)
    hg = pl.program_id(2)
    n_groups = pl.num_programs(2)
    group = wqa_ref.shape[0]
    tq = cqn_ref.shape[0]
    tk = tq
    nt = (((1,), (1,)), ((), ()))

    @pl.when(qi == 0)
    def _():
        ckv = ckvn_ref[...]
        for j in range(group):
            h = hg * group + j
            k_scr[h, :, 0:QK_NOPE] = jnp.dot(ckv, wk_ref[j], preferred_element_type=F32).astype(BF16)
            k_scr[h, :, QK_NOPE:2 * QK_NOPE] = kr_ref[...]
            vt = lax.dot_general(wvt_ref[j], ckv, nt, preferred_element_type=F32)
            for kb in range(vt_scr.shape[1]):
                vt_scr[h, kb] = vt[:, kb * tk:(kb + 1) * tk].astype(BF16)

    cq = cqn_ref[...]
    cs = cs_ref[...]
    ones = jnp.ones((tq, LANES), F32)
    m1 = jnp.concatenate([ones, cs[:, :LANES]], axis=-1) * scale
    m2 = jnp.concatenate([ones, cs[:, LANES:]], axis=-1) * scale
    qs = [(jnp.dot(cq, wqa_ref[j], preferred_element_type=F32) * m1
           + jnp.dot(cq, wqb_ref[j], preferred_element_type=F32) * m2).astype(BF16) for j in range(group)]

    def head_output(j, n_blocks):
        h = hg * group + j
        m = l = acc = None
        for kb in range(n_blocks):
            k = k_scr[h, kb * tk:(kb + 1) * tk, :]
            s = lax.dot_general(k, qs[j], nt, preferred_element_type=F32)
            if kb == n_blocks - 1:
                kpos = lax.broadcasted_iota(jnp.int32, (tk, tq), 0)
                qpos = lax.broadcasted_iota(jnp.int32, (tk, tq), 1)
                s = jnp.where(kpos <= qpos, s, MASK_VALUE)
            blk_max = jnp.max(s, axis=0, keepdims=True)
            m_new = blk_max if m is None else jnp.maximum(m, blk_max)
            p = jnp.exp(s - m_new)
            psum = jnp.sum(p, axis=0, keepdims=True)
            pv = jnp.dot(vt_scr[h, kb], p.astype(BF16), preferred_element_type=F32)
            if m is None:
                l, acc = psum, pv
            else:
                alpha = jnp.exp(m - m_new)
                l, acc = alpha * l + psum, alpha * acc + pv
            m = m_new
        return (acc / l).T

    for qv in range(vt_scr.shape[1]):
        @pl.when(qi == qv)
        def _(qv=qv):
            for j in range(group):
                o_scr[hg * group + j] = head_output(j, qv + 1)

    @pl.when(hg == n_groups - 1)
    def _():
        o = jnp.concatenate([o_scr[j] for j in range(N_MLA_HEADS)], axis=-1)
        out_ref[...] = _rms(o, gao_ref[...]).astype(BF16)


def _attention(cqn, cs, ckvn, kr, wqa, wqb, wk, wvt, g_attn_out, *, batch, seq):
    n, ql = cqn.shape
    kvl = ckvn.shape[1]
    tq = ATTN_TILE
    nq = seq // tq
    aw = N_MLA_HEADS * V_HEAD
    kern = functools.partial(_attn_kernel, scale=1.0 / math.sqrt(QK_NOPE + QK_ROPE))
    qrow = lambda c: pl.BlockSpec((tq, c), lambda b, qi, h: (b * nq + qi, 0))
    brow = lambda c: pl.BlockSpec((seq, c), lambda b, qi, h: (b, 0))
    group = ATTN_HEAD_GROUP
    head = lambda r, c: pl.BlockSpec((group, r, c), lambda b, qi, h: (h, 0, 0))
    return pl.pallas_call(
        kern,
        grid=(batch, nq, N_MLA_HEADS // group),
        in_specs=[qrow(ql), qrow(2 * LANES), brow(kvl), brow(LANES),
                  head(ql, 2 * QK_NOPE), head(ql, 2 * QK_NOPE), head(kvl, QK_NOPE), head(V_HEAD, kvl),
                  pl.BlockSpec((1, aw), lambda b, qi, h: (0, 0))],
        out_specs=qrow(aw),
        out_shape=jax.ShapeDtypeStruct((n, aw), BF16),
        scratch_shapes=[pltpu.VMEM((N_MLA_HEADS, seq, 2 * QK_NOPE), BF16),
                        pltpu.VMEM((N_MLA_HEADS, nq, V_HEAD, tq), BF16),
                        pltpu.VMEM((N_MLA_HEADS, tq, V_HEAD), F32)],
        compiler_params=_params(3),
        name="mla_attention",
    )(cqn, cs, ckvn, kr, wqa, wqb, wk, wvt, g_attn_out)


def _mem_kv_kernel(mem_ref, g_ref, w_ref, k_ref, v_ref):
    d = mem_ref.shape[1]
    mn = _rms(mem_ref[...], g_ref[...]).astype(BF16)
    k_ref[...] = jnp.dot(mn, w_ref[:, :d], preferred_element_type=F32).astype(BF16)
    v_ref[...] = jnp.dot(mn, w_ref[:, d:], preferred_element_type=F32).astype(BF16)


def _mem_kv(mem2d, g_mem, w_xkv, *, mem_len):
    n, d = mem2d.shape
    row = pl.BlockSpec((mem_len, d), lambda i: (i, 0))
    return pl.pallas_call(
        _mem_kv_kernel,
        grid=(n // mem_len,),
        in_specs=[row, _resident((1, d)), _resident((d, 2 * d))],
        out_specs=[row, row],
        out_shape=[jax.ShapeDtypeStruct((n, d), BF16)] * 2,
        compiler_params=_params(1),
        name="mem_kv",
    )(mem2d, g_mem, w_xkv)


def _route(logits):
    tm = logits.shape[0]
    lane = lax.broadcasted_iota(jnp.int32, (tm, LANES), 1)
    big = jnp.int32(1 << 20)
    ninf = -jnp.inf
    gl = jnp.where(lane < N_GROUPS, logits, ninf)
    gm = jnp.max(gl, axis=-1, keepdims=True)
    gidx = jnp.min(jnp.where(gl == gm, lane, big), axis=-1, keepdims=True)
    g_p = 1.0 / jnp.sum(jnp.exp(gl - gm), axis=-1, keepdims=True)
    first = N_GROUPS + gidx * EXPERTS_PER_GROUP
    el = jnp.where(lane >= first, jnp.where(lane < first + EXPERTS_PER_GROUP, logits, ninf), ninf)
    top1 = jnp.max(el, axis=-1, keepdims=True)
    i1 = jnp.min(jnp.where(el == top1, lane, big), axis=-1, keepdims=True)
    el2 = jnp.where(lane == i1, ninf, el)
    top2 = jnp.max(el2, axis=-1, keepdims=True)
    i2 = jnp.min(jnp.where(el2 == top2, lane, big), axis=-1, keepdims=True)
    r = jnp.exp(top2 - top1)
    w1 = g_p / (1.0 + r)
    w2 = g_p * r / (1.0 + r)
    e1 = (i1 - N_GROUPS).astype(F32)
    e2 = (i2 - N_GROUPS).astype(F32)
    return jnp.where(lane == 0, e1, jnp.where(lane == 1, e2,
                     jnp.where(lane == 2, w1, jnp.where(lane == 3, w2, 0.0))))


def _mid_kernel(x_ref, ycn_ref, yan_ref, wout_ref, gx_ref, wxq_ref, km_ref, vm_ref, wxo_ref,
                gmoe_ref, wrh_ref, wrl_ref, brt_ref, x2_ref, xp_ref, route_ref):
    d = x_ref.shape[1]
    cw = ycn_ref.shape[1]
    xd = d // N_X_HEADS
    x1 = (x_ref[...]
          + jnp.dot(ycn_ref[...], wout_ref[:cw, :], preferred_element_type=F32)
          + jnp.dot(yan_ref[...], wout_ref[cw:, :], preferred_element_type=F32))
    h2 = _rms(x1, gx_ref[...]).astype(BF16)
    q = (jnp.dot(h2, wxq_ref[...], preferred_element_type=F32) * (1.0 / math.sqrt(xd))).astype(BF16)
    heads = []
    for hh in range(N_X_HEADS):
        sl = slice(hh * xd, (hh + 1) * xd)
        s = lax.dot_general(q[:, sl], km_ref[:, sl], (((1,), (1,)), ((), ())),
                            preferred_element_type=F32)
        e = jnp.exp(s - jnp.max(s, axis=-1, keepdims=True))
        p = (e / jnp.sum(e, axis=-1, keepdims=True)).astype(BF16)
        heads.append(jnp.dot(p, vm_ref[:, sl], preferred_element_type=F32).astype(BF16))
    o = jnp.concatenate(heads, axis=-1)
    x2 = x1 + jnp.dot(o, wxo_ref[...], preferred_element_type=F32)
    x2_ref[...] = x2
    xn = _rms(x2, gmoe_ref[...])
    xn_hi = xn.astype(BF16)
    xn_lo = (xn - xn_hi.astype(F32)).astype(BF16)
    logits = (jnp.dot(xn_hi, wrh_ref[...], preferred_element_type=F32)
              + (jnp.dot(xn_lo, wrh_ref[...], preferred_element_type=F32)
                 + jnp.dot(xn_hi, wrl_ref[...], preferred_element_type=F32))) + brt_ref[...]
    route_ref[...] = _route(logits)
    _store_rows(xp_ref, xn)


def _mid(x2d, ycn, yan, w_out, g_xattn, w_xq, kmem, vmem, w_xo, g_moe, w_rt_hi, w_rt_lo, b_rt,
         *, seq, mem_len):
    n, d = x2d.shape
    cw = ycn.shape[1]
    tm = ROW_TILE
    sub = _row_sublanes(d)
    per_seq = seq // tm
    row = lambda c: pl.BlockSpec((tm, c), lambda i: (i, 0))
    memblk = pl.BlockSpec((mem_len, d), lambda i: (i // per_seq, 0))
    return pl.pallas_call(
        _mid_kernel,
        grid=(n // tm,),
        in_specs=[row(d), row(cw), row(yan.shape[1]), _resident((d, d)), _resident((1, d)),
                  _resident((d, d)), memblk, memblk, _resident((d, d)), _resident((1, d)),
                  _resident((d, LANES)), _resident((d, LANES)), _resident((1, LANES))],
        out_specs=[row(d), pl.BlockSpec((tm * sub, LANES), lambda i: (i, 0)), row(LANES)],
        out_shape=[jax.ShapeDtypeStruct((n, d), F32), jax.ShapeDtypeStruct((n * sub, LANES), F32),
                   jax.ShapeDtypeStruct((n, LANES), F32)],
        compiler_params=_params(1),
        name="mid",
    )(x2d, ycn, yan, w_out, g_xattn, w_xq, kmem, vmem, w_xo, g_moe, w_rt_hi, w_rt_lo, b_rt)


DMA_PRIORITIES = 2
GATHER_SLOTS = 3
TILE_SUBLANES = 8
GATHER_PITCH_PAD = 4


def _gather_pitch(sub):
    return sub + GATHER_PITCH_PAD


def _row_sublanes(d):
    sub = d // LANES
    assert sub % TILE_SUBLANES == 0, "a row must cover whole (8, 128) tiles"
    return sub


def _store_rows(dst_ref, v):
    rows, d = v.shape
    sub = _row_sublanes(d)
    for s in range(sub):
        dst_ref[pl.ds(s, rows, stride=sub), :] = v[:, s * LANES:(s + 1) * LANES]


def _load_gathered_rows(buf_ref, first_row, rows, sub):
    pitch = _gather_pitch(sub)
    return jnp.concatenate([buf_ref[pl.ds(first_row * pitch + s, rows, stride=pitch), :]
                            for s in range(sub)], axis=-1)


def _gather_rows(idx_ref, base, n_rows, src_hbm, dst_buf, sem, sub, inline=False, idx_shift=0):
    pitch = _gather_pitch(sub)

    def start_row(r, priority):
        t = idx_ref[base + r] >> idx_shift
        pltpu.make_async_copy(src_hbm.at[pl.ds(pl.multiple_of(t * sub, sub), sub)],
                              dst_buf.at[pl.ds(r * pitch, sub)], sem).start(priority=priority)

    if inline:
        for r in range(n_rows):
            start_row(r, r % DMA_PRIORITIES)
    else:
        def body(r2, carry):
            for p in range(DMA_PRIORITIES):
                start_row(r2 * DMA_PRIORITIES + p, p)
            return carry
        lax.fori_loop(0, n_rows // DMA_PRIORITIES, body, 0, unroll=4)


def _wait_rows(dst_buf, n_rows, sub, sem):
    span = dst_buf.at[pl.ds(0, n_rows * sub)]
    pltpu.make_async_copy(span, span, sem).wait()


def _experts_kernel(be_ref, na_ref, src_ref, nxt_ref, order_ref, xp_hbm, wg_hbm, wu_hbm, wd_hbm, y_ref,
                    xbuf, sem, stage_g, stage_u, stage_d, wsem, wgb, wub, wdb):
    i = pl.program_id(0)

    def weight_copies(e):
        return [pltpu.make_async_copy(w_hbm.at[e], stage, wsem.at[k])
                for k, (w_hbm, stage) in enumerate(((wg_hbm, stage_g), (wu_hbm, stage_u), (wd_hbm, stage_d)))]

    na = na_ref[0]
    n_slots = xbuf.shape[0]
    slot = i % n_slots
    ahead = n_slots - 1
    sub = _row_sublanes(wgb.shape[0])
    blk = xbuf.shape[1] // _gather_pitch(sub)
    tok_shift = TOP_K.bit_length() - 1

    def gather(b, dst_slot, inline):
        _gather_rows(order_ref, src_ref[b], blk, xp_hbm, xbuf.at[dst_slot], sem.at[dst_slot], sub,
                     inline=inline, idx_shift=tok_shift)

    @pl.when(i == 0)
    def _():
        for c in weight_copies(be_ref[0]):
            c.start()
        for b in range(ahead):
            gather(b, b, False)

    @pl.when(i < na + ahead)
    def _():
        _wait_rows(xbuf.at[slot], blk, sub, sem.at[slot])

    @pl.when(i < na)
    def _():
        new_expert = jnp.logical_or(i == 0, be_ref[i] != be_ref[jnp.maximum(i - 1, 0)])

        @pl.when(new_expert)
        def _():
            for c in weight_copies(be_ref[i]):
                c.wait()
            for stage, work in ((stage_g, wgb), (stage_u, wub), (stage_d, wdb)):
                chunk = 128

                def cast_chunk(c, carry, stage=stage, work=work):
                    rows = pl.ds(pl.multiple_of(c * chunk, chunk), chunk)
                    work[rows, :] = stage[rows, :].astype(BF16)
                    return carry
                lax.fori_loop(0, stage.shape[0] // chunk, cast_chunk, 0)

            @pl.when(nxt_ref[i] >= 0)
            def _():
                for c in weight_copies(nxt_ref[i]):
                    c.start()

        xb = _load_gathered_rows(xbuf.at[slot], 0, blk, sub).astype(BF16)
        gather(i + ahead, (i + ahead) % n_slots, True)
        g = jnp.dot(xb, wgb[...], preferred_element_type=F32)
        u = jnp.dot(xb, wub[...], preferred_element_type=F32)
        act = (g * jax.nn.sigmoid(g) * u).astype(BF16)
        _store_rows(y_ref, jnp.dot(act, wdb[...], preferred_element_type=F32))

    @pl.when(i >= na)
    def _():
        y_ref[...] = jnp.zeros_like(y_ref)


def _experts(blk_expert, n_active, blk_src, blk_next, order_pad, xp, w_gate, w_up, w_down):
    n_blk = blk_expert.shape[0]
    d = w_gate.shape[1]
    de = w_gate.shape[2]
    tile = (MOE_BLOCK * _row_sublanes(d), LANES)
    hbm = pl.BlockSpec(memory_space=pl.ANY)
    grid_spec = pltpu.PrefetchScalarGridSpec(
        num_scalar_prefetch=5,
        grid=(n_blk,),
        in_specs=[hbm, hbm, hbm, hbm],
        out_specs=pl.BlockSpec(tile, lambda i, *_: (i, 0)),
        scratch_shapes=[pltpu.VMEM((GATHER_SLOTS, MOE_BLOCK * _gather_pitch(_row_sublanes(d)), LANES), F32),
                        pltpu.SemaphoreType.DMA((GATHER_SLOTS,)),
                        pltpu.VMEM((d, de), F32), pltpu.VMEM((d, de), F32), pltpu.VMEM((de, d), F32),
                        pltpu.SemaphoreType.DMA((3,)),
                        pltpu.VMEM((d, de), BF16), pltpu.VMEM((d, de), BF16), pltpu.VMEM((de, d), BF16)],
    )
    return pl.pallas_call(
        _experts_kernel,
        grid_spec=grid_spec,
        out_shape=jax.ShapeDtypeStruct((n_blk * tile[0], LANES), F32),
        compiler_params=_params(1),
        name="experts",
    )(blk_expert, n_active, blk_src, blk_next, order_pad, xp, w_gate, w_up, w_down)


def _combine_kernel(pos_ref, x2_ref, route_ref, g_ref, y_hbm, out_ref, ybuf, sem):
    i = pl.program_id(0)
    n = pl.num_programs(0)
    slot = i % 2
    tm = x2_ref.shape[0]
    rows = TOP_K * tm
    sub = _row_sublanes(x2_ref.shape[1])

    @pl.when(i == 0)
    def _():
        _gather_rows(pos_ref, 0, rows, y_hbm, ybuf.at[0], sem.at[0], sub)

    _wait_rows(ybuf.at[slot], rows, sub, sem.at[slot])
    ys = [_load_gathered_rows(ybuf.at[slot], k * tm, tm, sub) for k in range(TOP_K)]
    nxt = jnp.minimum(i + 1, n - 1)
    _gather_rows(pos_ref, nxt * rows, rows, y_hbm, ybuf.at[1 - slot], sem.at[1 - slot], sub, inline=True)
    route = route_ref[...]
    moe = None
    for k in range(TOP_K):
        term = ys[k] * route[:, TOP_K + k:TOP_K + k + 1]
        moe = term if moe is None else moe + term
    out_ref[...] = _rms(x2_ref[...] + moe, g_ref[...])

    @pl.when(i == n - 1)
    def _():
        _wait_rows(ybuf.at[1 - slot], rows, sub, sem.at[1 - slot])


def _combine(pos, x2, route, g_final, y):
    n, d = x2.shape
    tm = ROW_TILE
    grid_spec = pltpu.PrefetchScalarGridSpec(
        num_scalar_prefetch=1,
        grid=(n // tm,),
        in_specs=[pl.BlockSpec((tm, d), lambda i, pos: (i, 0)),
                  pl.BlockSpec((tm, LANES), lambda i, pos: (i, 0)),
                  pl.BlockSpec((1, d), lambda i, pos: (0, 0)),
                  pl.BlockSpec(memory_space=pl.ANY)],
        out_specs=pl.BlockSpec((tm, d), lambda i, pos: (i, 0)),
        scratch_shapes=[pltpu.VMEM((2, TOP_K * tm * _gather_pitch(_row_sublanes(d)), LANES), F32),
                        pltpu.SemaphoreType.DMA((2,))],
    )
    return pl.pallas_call(
        _combine_kernel,
        grid_spec=grid_spec,
        out_shape=jax.ShapeDtypeStruct((n, d), F32),
        compiler_params=_params(1),
        name="combine",
    )(pos, x2, route, g_final, y)


def _dispatch(expert_id):
    n_tok = expert_id.shape[0]
    n_slot = n_tok * TOP_K
    flat_e = expert_id.reshape(-1)
    iota_slot = jnp.arange(n_slot, dtype=jnp.int32)
    s_e, order = lax.sort((flat_e, iota_slot), num_keys=1)
    experts = jnp.arange(N_EXPERTS, dtype=jnp.int32)
    counts = jnp.sum((flat_e[:, None] == experts[None, :]).astype(jnp.int32), axis=0)
    start = jnp.cumsum(counts) - counts
    padded = ((counts + MOE_BLOCK - 1) // MOE_BLOCK) * MOE_BLOCK
    pad_end = jnp.cumsum(padded)
    pad_start = pad_end - padded
    n_blk = n_slot // MOE_BLOCK + N_EXPERTS + GATHER_SLOTS - 2
    n_active = (pad_end[-1] // MOE_BLOCK).astype(jnp.int32)
    blk_ids = jnp.arange(n_blk, dtype=jnp.int32)
    blk_first = blk_ids * MOE_BLOCK
    blk_expert = jnp.minimum(jnp.sum((pad_end[None, :] <= blk_first[:, None]).astype(jnp.int32), axis=1),
                             N_EXPERTS - 1)
    blk_expert = jnp.where(blk_ids < n_active, blk_expert, blk_expert[n_active - 1])
    shift = pad_start - start
    onehot = (blk_expert[:, None] == experts[None, :]).astype(jnp.int32)
    blk_src = jnp.clip(blk_first - jnp.sum(onehot * shift[None, :], axis=1), 0, n_slot)
    is_first = (blk_ids < n_active) & (blk_first == jnp.sum(onehot * pad_start[None, :], axis=1))
    nxt_blk = blk_ids + jnp.sum(onehot * (padded // MOE_BLOCK)[None, :], axis=1)
    blk_next = jnp.where(is_first & (nxt_blk < n_active),
                         blk_expert[jnp.minimum(nxt_blk, n_blk - 1)], -1).astype(jnp.int32)
    order_pad = jnp.concatenate([order, jnp.zeros((MOE_BLOCK,), jnp.int32)])
    shift_sorted = jnp.sum((s_e[:, None] == experts[None, :]).astype(jnp.int32) * shift[None, :], axis=1)
    _, pos = lax.sort((order, iota_slot + shift_sorted), num_keys=1)
    return blk_expert, n_active.reshape(1), blk_src, blk_next, order_pad, pos


def _rotate_half_cols(w):
    half = w.shape[-1] // 2
    return jnp.concatenate([-w[..., half:], w[..., :half]], axis=-1)


def _layer(x, mem, positions, g_mix, w_in, w_conv, g_q_a, w_q_up, g_kv_a, w_kv_up, g_conv_out,
           g_attn_out, w_out, g_xattn, g_mem, w_xq, w_xk, w_xv, w_xo, g_moe, w_group, b_group,
           w_router, b_router, w_gate, w_up, w_down):
    batch, seq, d = x.shape
    mem_len = mem.shape[1]
    n = batch * seq
    cw = w_conv.shape[1]
    ql = g_q_a.shape[0]
    kvl = g_kv_a.shape[0]
    c_rope = 3 * cw + ql + kvl
    row2 = lambda g: g.reshape(1, -1)

    w_kr = w_in[:, c_rope:c_rope + QK_ROPE]
    w_in_ext = jnp.concatenate([w_in[:, :c_rope + QK_ROPE], _rotate_half_cols(w_kr)], axis=1).astype(BF16)
    inv_freq = 1.0 / (ROPE_THETA ** (jnp.arange(0, QK_ROPE, 2, dtype=F32) / QK_ROPE))
    freq = jnp.tile(inv_freq, LANES // inv_freq.shape[0]).reshape(1, LANES)

    ycn, cqn, ckvn, kr, cs = _in_proj(
        x.reshape(n, d), positions.reshape(n, 1), row2(g_mix), w_in_ext, w_conv, row2(g_q_a),
        row2(g_kv_a), row2(g_conv_out), freq, seq=seq, cw=cw, ql=ql, kvl=kvl)

    wq = w_q_up.reshape(ql, N_MLA_HEADS, QK_NOPE + QK_ROPE).transpose(1, 0, 2)
    wq_nope, wq_rope = wq[..., :QK_NOPE], wq[..., QK_NOPE:]
    z64 = jnp.zeros((N_MLA_HEADS, ql, QK_ROPE), F32)
    wqa = jnp.concatenate([wq_nope, wq_rope, z64], axis=-1).astype(BF16)
    wqb = jnp.concatenate([jnp.zeros_like(wq_nope), _rotate_half_cols(wq_rope), z64], axis=-1).astype(BF16)
    wkv = w_kv_up.reshape(kvl, N_MLA_HEADS, QK_NOPE + V_HEAD).transpose(1, 0, 2).astype(BF16)
    wk = wkv[..., :QK_NOPE]
    wvt = wkv[..., QK_NOPE:].transpose(0, 2, 1)

    yan = _attention(cqn, cs, ckvn, kr, wqa, wqb, wk, wvt, row2(g_attn_out), batch=batch, seq=seq)

    w_xkv = jnp.concatenate([w_xk, w_xv], axis=1).astype(BF16)
    kmem, vmem = _mem_kv(mem.reshape(batch * mem_len, d), row2(g_mem), w_xkv, mem_len=mem_len)

    pad = LANES - N_GROUPS - N_EXPERTS
    w_rt = jnp.concatenate([w_group, w_router, jnp.zeros((d, pad), F32)], axis=1)
    w_rt_hi = w_rt.astype(BF16)
    w_rt_lo = (w_rt - w_rt_hi.astype(F32)).astype(BF16)
    b_rt = jnp.concatenate([b_group, b_router, jnp.zeros((pad,), F32)]).reshape(1, LANES)
    x2, xp, route = _mid(x.reshape(n, d), ycn, yan, w_out.astype(BF16), row2(g_xattn), w_xq.astype(BF16),
                         kmem, vmem, w_xo.astype(BF16), row2(g_moe), w_rt_hi, w_rt_lo, b_rt,
                         seq=seq, mem_len=mem_len)

    expert_id = route[:, :TOP_K].astype(jnp.int32)
    blk_expert, n_active, blk_src, blk_next, order_pad, pos = _dispatch(expert_id)
    y = _experts(blk_expert, n_active, blk_src, blk_next, order_pad, xp, w_gate, w_up, w_down)
    pos_tiles = pos.reshape(n // ROW_TILE, ROW_TILE, TOP_K).transpose(0, 2, 1).reshape(-1)
    return x2, route, pos_tiles, y


def kernel(x, mem, positions, g_mix, w_in, w_conv, g_q_a, w_q_up, g_kv_a, w_kv_up, g_conv_out, g_attn_out, w_out, g_xattn, g_mem, w_xq, w_xk, w_xv, w_xo, g_moe, w_group, b_group, w_router, b_router, w_gate, w_up, w_down, g_final):
    depth = g_mix.shape[0]
    assert depth == 1, "the combine stage applies the final norm, so exactly one layer is supported"
    shape = x.shape
    x2, route, pos_tiles, y = _layer(
        x, mem, positions, g_mix[0], w_in[0], w_conv[0], g_q_a[0], w_q_up[0], g_kv_a[0], w_kv_up[0],
        g_conv_out[0], g_attn_out[0], w_out[0], g_xattn[0], g_mem[0], w_xq[0], w_xk[0], w_xv[0],
        w_xo[0], g_moe[0], w_group[0], b_group[0], w_router[0], b_router[0], w_gate[0], w_up[0], w_down[0])
    out = _combine(pos_tiles, x2, route, g_final.reshape(1, -1), y)
    return out.reshape(shape)
```

```python
import functools
import math

import jax
import jax.numpy as jnp
from jax import lax
from jax.experimental import pallas as pl
from jax.experimental.pallas import tpu as pltpu

F32 = jnp.float32
BF16 = jnp.bfloat16

RMS_EPS = 1e-6
N_MLA_HEADS = 8
QK_NOPE = 128
QK_ROPE = 64
V_HEAD = 128
ROPE_THETA = 10000.0
MASK_VALUE = -1e30
N_X_HEADS = 4
N_GROUPS = 8
EXPERTS_PER_GROUP = 8
N_EXPERTS = N_GROUPS * EXPERTS_PER_GROUP
TOP_K = 2
MOE_BLOCK = 256

LANES = 128
VMEM_LIMIT_BYTES = 60000 * 1024

IN_PROJ_TILE = 512
ROW_TILE = 256
ATTN_TILE = 512
ATTN_KV_TILE = 512
ATTN_HEAD_GROUP = 2


def _rms(x, g):
    ms = jnp.mean(x * x, axis=-1, keepdims=True)
    return x * lax.rsqrt(ms + RMS_EPS) * g


def _resident(shape):
    nd = len(shape)
    return pl.BlockSpec(shape, lambda *_: (0,) * nd, pipeline_mode=pl.Buffered(1))


def _params(n_axes):
    return pltpu.CompilerParams(dimension_semantics=("arbitrary",) * n_axes,
                                vmem_limit_bytes=VMEM_LIMIT_BYTES)


def _in_proj_kernel(x_ref, pos_ref, gmix_ref, w_ref, wconv_ref, gq_ref, gkv_ref, gco_ref, freq_ref,
                    ycn_ref, cqn_ref, ckvn_ref, kr_ref, cs_ref, carry_ref,
                    *, tiles_per_seq, cw, ql, kvl):
    i = pl.program_id(0)
    tm = x_ref.shape[0]
    xn = _rms(x_ref[...], gmix_ref[...]).astype(BF16)

    def mm(lo, hi):
        return jnp.dot(xn, w_ref[:, lo:hi], preferred_element_type=F32)

    zb, zc, zh = mm(0, cw), mm(cw, 2 * cw), mm(2 * cw, 3 * cw)
    u = zc * zh

    @pl.when(i % tiles_per_seq == 0)
    def _():
        carry_ref[...] = jnp.zeros_like(carry_ref)

    prev = carry_ref[...]
    carry_ref[...] = u[tm - 8:, :]
    row = lax.broadcasted_iota(jnp.int32, (tm, 1), 0)
    u1 = jnp.where(row == 0, prev[7:8, :], pltpu.roll(u, 1, axis=0))
    u2 = jnp.where(row == 0, prev[6:7, :],
                   jnp.where(row == 1, prev[7:8, :], pltpu.roll(u, 2, axis=0)))
    wc = wconv_ref[...]
    conv = wc[0:1, :] * u2 + wc[1:2, :] * u1 + wc[2:3, :] * u
    ycn_ref[...] = _rms(zb * conv, gco_ref[...]).astype(BF16)

    c3 = 3 * cw
    cqn_ref[...] = _rms(mm(c3, c3 + ql), gq_ref[...]).astype(BF16)
    ckvn_ref[...] = _rms(mm(c3 + ql, c3 + ql + kvl), gkv_ref[...]).astype(BF16)

    zk = mm(c3 + ql + kvl, c3 + ql + kvl + LANES)
    ang = pos_ref[...].astype(F32) * freq_ref[...]
    cos, sin = jnp.cos(ang), jnp.sin(ang)
    lane = lax.broadcasted_iota(jnp.int32, (tm, LANES), 1)
    lo = lane < QK_ROPE
    cos_sin = jnp.where(lo, cos, sin)
    t = zk * cos_sin
    kr = t + pltpu.roll(t, QK_ROPE, axis=1)
    kr_ref[...] = jnp.where(lo, kr, 0.0).astype(BF16)
    cs_ref[...] = jnp.concatenate([cos_sin, jnp.where(lo, sin, cos)], axis=-1)


def _in_proj(x2d, pos2d, g_mix, w_in_ext, w_conv, g_q_a, g_kv_a, g_conv_out, freq, *, seq, cw, ql, kvl):
    n, d = x2d.shape
    tm = IN_PROJ_TILE
    wcols = w_in_ext.shape[1]
    kern = functools.partial(_in_proj_kernel, tiles_per_seq=seq // tm, cw=cw, ql=ql, kvl=kvl)
    row = lambda c: pl.BlockSpec((tm, c), lambda i: (i, 0))
    return pl.pallas_call(
        kern,
        grid=(n // tm,),
        in_specs=[row(d), row(1), _resident((1, d)), _resident((d, wcols)), _resident((3, cw)),
                  _resident((1, ql)), _resident((1, kvl)), _resident((1, cw)), _resident((1, LANES))],
        out_specs=[row(cw), row(ql), row(kvl), row(LANES), row(2 * LANES)],
        out_shape=[jax.ShapeDtypeStruct((n, cw), BF16), jax.ShapeDtypeStruct((n, ql), BF16),
                   jax.ShapeDtypeStruct((n, kvl), BF16), jax.ShapeDtypeStruct((n, LANES), BF16),
                   jax.ShapeDtypeStruct((n, 2 * LANES), F32)],
        scratch_shapes=[pltpu.VMEM((8, cw), F32)],
        compiler_params=_params(1),
        name="in_proj",
    )(x2d, pos2d, g_mix, w_in_ext, w_conv, g_q_a, g_kv_a, g_conv_out, freq)


def _attn_kernel(cqn_ref, cs_ref, ckvn_ref, kr_ref, wqa_ref, wqb_ref, wk_ref, wvt_ref, gao_ref,
                 out_ref, k_scr, vt_scr, o_scr, *, scale):
    qi = pl.program_id(1)
    hg = pl.program_id(2)
    n_groups = pl.num_programs(2)
    group = wqa_ref.shape[0]
    tq = cqn_ref.shape[0]
    tk = vt_scr.shape[3]
    nt = (((1,), (1,)), ((), ()))

    @pl.when(qi == 0)
    def _():
        ckv = ckvn_ref[...]
        for j in range(group):
            h = hg * group + j
            k_scr[h, :, 0:QK_NOPE] = jnp.dot(ckv, wk_ref[j], preferred_element_type=F32).astype(BF16)
            k_scr[h, :, QK_NOPE:2 * QK_NOPE] = kr_ref[...]
            vt = lax.dot_general(wvt_ref[j], ckv, nt, preferred_element_type=F32)
            for kb in range(vt_scr.shape[1]):
                vt_scr[h, kb] = vt[:, kb * tk:(kb + 1) * tk].astype(BF16)

    cq = cqn_ref[...]
    cs = cs_ref[...]
    ones = jnp.ones((tq, LANES), F32)
    m1 = jnp.concatenate([ones, cs[:, :LANES]], axis=-1) * scale
    m2 = jnp.concatenate([ones, cs[:, LANES:]], axis=-1) * scale
    qs = [(jnp.dot(cq, wqa_ref[j], preferred_element_type=F32) * m1
           + jnp.dot(cq, wqb_ref[j], preferred_element_type=F32) * m2).astype(BF16) for j in range(group)]

    def head_output(j, qv):
        h = hg * group + j
        n_kb = (qv + 1) * tq // tk

        def scores(kb):
            k = k_scr[h, kb * tk:(kb + 1) * tk, :]
            s = lax.dot_general(k, qs[j], nt, preferred_element_type=F32)
            if (kb + 1) * tk > qv * tq:
                kpos = lax.broadcasted_iota(jnp.int32, (tk, tq), 0) + kb * tk
                qpos = lax.broadcasted_iota(jnp.int32, (tk, tq), 1) + qv * tq
                s = jnp.where(kpos <= qpos, s, MASK_VALUE)
            return s

        m = None
        for kb in range(n_kb):
            blk_max = jnp.max(scores(kb), axis=0, keepdims=True)
            m = blk_max if m is None else jnp.maximum(m, blk_max)
        l = acc = None
        for kb in range(n_kb):
            p = jnp.exp(scores(kb) - m)
            psum = jnp.sum(p, axis=0, keepdims=True)
            pv = jnp.dot(vt_scr[h, kb], p.astype(BF16), preferred_element_type=F32)
            l, acc = (psum, pv) if l is None else (l + psum, acc + pv)
        return (acc / l).T

    for qv in range(vt_scr.shape[1] * tk // tq):
        @pl.when(qi == qv)
        def _(qv=qv):
            for j in range(group):
                o_scr[hg * group + j] = head_output(j, qv)

    @pl.when(hg == n_groups - 1)
    def _():
        o = jnp.concatenate([o_scr[j] for j in range(N_MLA_HEADS)], axis=-1)
        out_ref[...] = _rms(o, gao_ref[...]).astype(BF16)


def _attention(cqn, cs, ckvn, kr, wqa, wqb, wk, wvt, g_attn_out, *, batch, seq):
    n, ql = cqn.shape
    kvl = ckvn.shape[1]
    tq = ATTN_TILE
    nq = seq // tq
    aw = N_MLA_HEADS * V_HEAD
    kern = functools.partial(_attn_kernel, scale=1.0 / math.sqrt(QK_NOPE + QK_ROPE))
    qrow = lambda c: pl.BlockSpec((tq, c), lambda b, qi, h: (b * nq + qi, 0))
    brow = lambda c: pl.BlockSpec((seq, c), lambda b, qi, h: (b, 0))
    group = ATTN_HEAD_GROUP
    head = lambda r, c: pl.BlockSpec((group, r, c), lambda b, qi, h: (h, 0, 0))
    return pl.pallas_call(
        kern,
        grid=(batch, nq, N_MLA_HEADS // group),
        in_specs=[qrow(ql), qrow(2 * LANES), brow(kvl), brow(LANES),
                  head(ql, 2 * QK_NOPE), head(ql, 2 * QK_NOPE), head(kvl, QK_NOPE), head(V_HEAD, kvl),
                  pl.BlockSpec((1, aw), lambda b, qi, h: (0, 0))],
        out_specs=qrow(aw),
        out_shape=jax.ShapeDtypeStruct((n, aw), BF16),
        scratch_shapes=[pltpu.VMEM((N_MLA_HEADS, seq, 2 * QK_NOPE), BF16),
                        pltpu.VMEM((N_MLA_HEADS, seq // ATTN_KV_TILE, V_HEAD, ATTN_KV_TILE), BF16),
                        pltpu.VMEM((N_MLA_HEADS, tq, V_HEAD), F32)],
        compiler_params=_params(3),
        name="mla_attention",
    )(cqn, cs, ckvn, kr, wqa, wqb, wk, wvt, g_attn_out)


def _mem_kv_kernel(mem_ref, g_ref, w_ref, k_ref, v_ref):
    d = mem_ref.shape[1]
    mn = _rms(mem_ref[...], g_ref[...]).astype(BF16)
    k_ref[...] = jnp.dot(mn, w_ref[:, :d], preferred_element_type=F32).astype(BF16)
    v_ref[...] = jnp.dot(mn, w_ref[:, d:], preferred_element_type=F32).astype(BF16)


def _mem_kv(mem2d, g_mem, w_xkv, *, mem_len):
    n, d = mem2d.shape
    row = pl.BlockSpec((mem_len, d), lambda i: (i, 0))
    return pl.pallas_call(
        _mem_kv_kernel,
        grid=(n // mem_len,),
        in_specs=[row, _resident((1, d)), _resident((d, 2 * d))],
        out_specs=[row, row],
        out_shape=[jax.ShapeDtypeStruct((n, d), BF16)] * 2,
        compiler_params=_params(1),
        name="mem_kv",
    )(mem2d, g_mem, w_xkv)


def _route(logits):
    tm = logits.shape[0]
    lane = lax.broadcasted_iota(jnp.int32, (tm, LANES), 1)
    big = jnp.int32(1 << 20)
    ninf = -jnp.inf
    gl = jnp.where(lane < N_GROUPS, logits, ninf)
    gm = jnp.max(gl, axis=-1, keepdims=True)
    gidx = jnp.min(jnp.where(gl == gm, lane, big), axis=-1, keepdims=True)
    g_p = 1.0 / jnp.sum(jnp.exp(gl - gm), axis=-1, keepdims=True)
    first = N_GROUPS + gidx * EXPERTS_PER_GROUP
    el = jnp.where(lane >= first, jnp.where(lane < first + EXPERTS_PER_GROUP, logits, ninf), ninf)
    top1 = jnp.max(el, axis=-1, keepdims=True)
    i1 = jnp.min(jnp.where(el == top1, lane, big), axis=-1, keepdims=True)
    el2 = jnp.where(lane == i1, ninf, el)
    top2 = jnp.max(el2, axis=-1, keepdims=True)
    i2 = jnp.min(jnp.where(el2 == top2, lane, big), axis=-1, keepdims=True)
    r = jnp.exp(top2 - top1)
    w1 = g_p / (1.0 + r)
    w2 = g_p * r / (1.0 + r)
    e1 = (i1 - N_GROUPS).astype(F32)
    e2 = (i2 - N_GROUPS).astype(F32)
    return jnp.where(lane == 0, e1, jnp.where(lane == 1, e2,
                     jnp.where(lane == 2, w1, jnp.where(lane == 3, w2, 0.0))))


def _mid_kernel(x_ref, ycn_ref, yan_ref, wout_ref, gx_ref, wxq_ref, km_ref, vm_ref, wxo_ref,
                gmoe_ref, wrh_ref, wrl_ref, brt_ref, x2_ref, xp_ref, route_ref):
    d = x_ref.shape[1]
    cw = ycn_ref.shape[1]
    xd = d // N_X_HEADS
    x1 = (x_ref[...]
          + jnp.dot(ycn_ref[...], wout_ref[:cw, :], preferred_element_type=F32)
          + jnp.dot(yan_ref[...], wout_ref[cw:, :], preferred_element_type=F32))
    h2 = _rms(x1, gx_ref[...]).astype(BF16)
    q = (jnp.dot(h2, wxq_ref[...], preferred_element_type=F32) * (1.0 / math.sqrt(xd))).astype(BF16)
    heads = []
    for hh in range(N_X_HEADS):
        sl = slice(hh * xd, (hh + 1) * xd)
        s = lax.dot_general(q[:, sl], km_ref[:, sl], (((1,), (1,)), ((), ())),
                            preferred_element_type=F32)
        e = jnp.exp(s - jnp.max(s, axis=-1, keepdims=True))
        p = (e / jnp.sum(e, axis=-1, keepdims=True)).astype(BF16)
        heads.append(jnp.dot(p, vm_ref[:, sl], preferred_element_type=F32).astype(BF16))
    o = jnp.concatenate(heads, axis=-1)
    x2 = x1 + jnp.dot(o, wxo_ref[...], preferred_element_type=F32)
    x2_ref[...] = x2
    xn = _rms(x2, gmoe_ref[...])
    xn_hi = xn.astype(BF16)
    xn_lo = (xn - xn_hi.astype(F32)).astype(BF16)
    logits = (jnp.dot(xn_hi, wrh_ref[...], preferred_element_type=F32)
              + (jnp.dot(xn_lo, wrh_ref[...], preferred_element_type=F32)
                 + jnp.dot(xn_hi, wrl_ref[...], preferred_element_type=F32))) + brt_ref[...]
    route_ref[...] = _route(logits)
    _store_rows(xp_ref, xn)


def _mid(x2d, ycn, yan, w_out, g_xattn, w_xq, kmem, vmem, w_xo, g_moe, w_rt_hi, w_rt_lo, b_rt,
         *, seq, mem_len):
    n, d = x2d.shape
    cw = ycn.shape[1]
    tm = ROW_TILE
    sub = _row_sublanes(d)
    per_seq = seq // tm
    row = lambda c: pl.BlockSpec((tm, c), lambda i: (i, 0))
    memblk = pl.BlockSpec((mem_len, d), lambda i: (i // per_seq, 0))
    return pl.pallas_call(
        _mid_kernel,
        grid=(n // tm,),
        in_specs=[row(d), row(cw), row(yan.shape[1]), _resident((d, d)), _resident((1, d)),
                  _resident((d, d)), memblk, memblk, _resident((d, d)), _resident((1, d)),
                  _resident((d, LANES)), _resident((d, LANES)), _resident((1, LANES))],
        out_specs=[row(d), pl.BlockSpec((tm * sub, LANES), lambda i: (i, 0)), row(LANES)],
        out_shape=[jax.ShapeDtypeStruct((n, d), F32), jax.ShapeDtypeStruct((n * sub, LANES), F32),
                   jax.ShapeDtypeStruct((n, LANES), F32)],
        compiler_params=_params(1),
        name="mid",
    )(x2d, ycn, yan, w_out, g_xattn, w_xq, kmem, vmem, w_xo, g_moe, w_rt_hi, w_rt_lo, b_rt)


DMA_PRIORITIES = 2
GATHER_SLOTS = 3
TILE_SUBLANES = 8
GATHER_PITCH_PAD = 4


def _gather_pitch(sub):
    return sub + GATHER_PITCH_PAD


def _row_sublanes(d):
    sub = d // LANES
    assert sub % TILE_SUBLANES == 0, "a row must cover whole (8, 128) tiles"
    return sub


def _store_rows(dst_ref, v):
    rows, d = v.shape
    sub = _row_sublanes(d)
    for s in range(sub):
        dst_ref[pl.ds(s, rows, stride=sub), :] = v[:, s * LANES:(s + 1) * LANES]


def _load_gathered_rows(buf_ref, first_row, rows, sub):
    pitch = _gather_pitch(sub)
    return jnp.concatenate([buf_ref[pl.ds(first_row * pitch + s, rows, stride=pitch), :]
                            for s in range(sub)], axis=-1)


def _gather_rows(idx_ref, base, n_rows, src_hbm, dst_buf, sem, sub, inline=False, idx_shift=0):
    pitch = _gather_pitch(sub)

    def start_row(r, priority):
        t = idx_ref[base + r] >> idx_shift
        pltpu.make_async_copy(src_hbm.at[pl.ds(pl.multiple_of(t * sub, sub), sub)],
                              dst_buf.at[pl.ds(r * pitch, sub)], sem).start(priority=priority)

    if inline:
        for r in range(n_rows):
            start_row(r, r % DMA_PRIORITIES)
    else:
        def body(r2, carry):
            for p in range(DMA_PRIORITIES):
                start_row(r2 * DMA_PRIORITIES + p, p)
            return carry
        lax.fori_loop(0, n_rows // DMA_PRIORITIES, body, 0, unroll=4)


def _wait_rows(dst_buf, n_rows, sub, sem):
    span = dst_buf.at[pl.ds(0, n_rows * sub)]
    pltpu.make_async_copy(span, span, sem).wait()


def _experts_kernel(be_ref, na_ref, src_ref, nxt_ref, order_ref, xp_hbm, wg_hbm, wu_hbm, wd_hbm, y_ref,
                    xbuf, sem, stage_g, stage_u, stage_d, wsem, wgb, wub, wdb):
    i = pl.program_id(0)

    def weight_copies(e):
        return [pltpu.make_async_copy(w_hbm.at[e], stage, wsem.at[k])
                for k, (w_hbm, stage) in enumerate(((wg_hbm, stage_g), (wu_hbm, stage_u), (wd_hbm, stage_d)))]

    na = na_ref[0]
    n_slots = xbuf.shape[0]
    slot = i % n_slots
    ahead = n_slots - 1
    sub = _row_sublanes(wgb.shape[0])
    blk = xbuf.shape[1] // _gather_pitch(sub)
    tok_shift = TOP_K.bit_length() - 1

    def gather(b, dst_slot, inline):
        _gather_rows(order_ref, src_ref[b], blk, xp_hbm, xbuf.at[dst_slot], sem.at[dst_slot], sub,
                     inline=inline, idx_shift=tok_shift)

    @pl.when(i == 0)
    def _():
        for c in weight_copies(be_ref[0]):
            c.start()
        for b in range(ahead):
            gather(b, b, False)

    @pl.when(i < na + ahead)
    def _():
        _wait_rows(xbuf.at[slot], blk, sub, sem.at[slot])

    @pl.when(i < na)
    def _():
        new_expert = jnp.logical_or(i == 0, be_ref[i] != be_ref[jnp.maximum(i - 1, 0)])

        @pl.when(new_expert)
        def _():
            for c in weight_copies(be_ref[i]):
                c.wait()
            for stage, work in ((stage_g, wgb), (stage_u, wub), (stage_d, wdb)):
                chunk = 128

                def cast_chunk(c, carry, stage=stage, work=work):
                    rows = pl.ds(pl.multiple_of(c * chunk, chunk), chunk)
                    work[rows, :] = stage[rows, :].astype(BF16)
                    return carry
                lax.fori_loop(0, stage.shape[0] // chunk, cast_chunk, 0)

            @pl.when(nxt_ref[i] >= 0)
            def _():
                for c in weight_copies(nxt_ref[i]):
                    c.start()

        xb = _load_gathered_rows(xbuf.at[slot], 0, blk, sub).astype(BF16)
        gather(i + ahead, (i + ahead) % n_slots, True)
        g = jnp.dot(xb, wgb[...], preferred_element_type=F32)
        u = jnp.dot(xb, wub[...], preferred_element_type=F32)
        act = (g * jax.nn.sigmoid(g) * u).astype(BF16)
        _store_rows(y_ref, jnp.dot(act, wdb[...], preferred_element_type=F32))

    @pl.when(i >= na)
    def _():
        y_ref[...] = jnp.zeros_like(y_ref)


def _experts(blk_expert, n_active, blk_src, blk_next, order_pad, xp, w_gate, w_up, w_down):
    n_blk = blk_expert.shape[0]
    d = w_gate.shape[1]
    de = w_gate.shape[2]
    tile = (MOE_BLOCK * _row_sublanes(d), LANES)
    hbm = pl.BlockSpec(memory_space=pl.ANY)
    grid_spec = pltpu.PrefetchScalarGridSpec(
        num_scalar_prefetch=5,
        grid=(n_blk,),
        in_specs=[hbm, hbm, hbm, hbm],
        out_specs=pl.BlockSpec(tile, lambda i, *_: (i, 0)),
        scratch_shapes=[pltpu.VMEM((GATHER_SLOTS, MOE_BLOCK * _gather_pitch(_row_sublanes(d)), LANES), F32),
                        pltpu.SemaphoreType.DMA((GATHER_SLOTS,)),
                        pltpu.VMEM((d, de), F32), pltpu.VMEM((d, de), F32), pltpu.VMEM((de, d), F32),
                        pltpu.SemaphoreType.DMA((3,)),
                        pltpu.VMEM((d, de), BF16), pltpu.VMEM((d, de), BF16), pltpu.VMEM((de, d), BF16)],
    )
    return pl.pallas_call(
        _experts_kernel,
        grid_spec=grid_spec,
        out_shape=jax.ShapeDtypeStruct((n_blk * tile[0], LANES), F32),
        compiler_params=_params(1),
        name="experts",
    )(blk_expert, n_active, blk_src, blk_next, order_pad, xp, w_gate, w_up, w_down)


def _combine_kernel(pos_ref, x2_ref, route_ref, g_ref, y_hbm, out_ref, ybuf, sem):
    i = pl.program_id(0)
    n = pl.num_programs(0)
    slot = i % 2
    tm = x2_ref.shape[0]
    rows = TOP_K * tm
    sub = _row_sublanes(x2_ref.shape[1])

    @pl.when(i == 0)
    def _():
        _gather_rows(pos_ref, 0, rows, y_hbm, ybuf.at[0], sem.at[0], sub)

    _wait_rows(ybuf.at[slot], rows, sub, sem.at[slot])
    ys = [_load_gathered_rows(ybuf.at[slot], k * tm, tm, sub) for k in range(TOP_K)]
    nxt = jnp.minimum(i + 1, n - 1)
    _gather_rows(pos_ref, nxt * rows, rows, y_hbm, ybuf.at[1 - slot], sem.at[1 - slot], sub, inline=True)
    route = route_ref[...]
    moe = None
    for k in range(TOP_K):
        term = ys[k] * route[:, TOP_K + k:TOP_K + k + 1]
        moe = term if moe is None else moe + term
    out_ref[...] = _rms(x2_ref[...] + moe, g_ref[...])

    @pl.when(i == n - 1)
    def _():
        _wait_rows(ybuf.at[1 - slot], rows, sub, sem.at[1 - slot])


def _combine(pos, x2, route, g_final, y):
    n, d = x2.shape
    tm = ROW_TILE
    grid_spec = pltpu.PrefetchScalarGridSpec(
        num_scalar_prefetch=1,
        grid=(n // tm,),
        in_specs=[pl.BlockSpec((tm, d), lambda i, pos: (i, 0)),
                  pl.BlockSpec((tm, LANES), lambda i, pos: (i, 0)),
                  pl.BlockSpec((1, d), lambda i, pos: (0, 0)),
                  pl.BlockSpec(memory_space=pl.ANY)],
        out_specs=pl.BlockSpec((tm, d), lambda i, pos: (i, 0)),
        scratch_shapes=[pltpu.VMEM((2, TOP_K * tm * _gather_pitch(_row_sublanes(d)), LANES), F32),
                        pltpu.SemaphoreType.DMA((2,))],
    )
    return pl.pallas_call(
        _combine_kernel,
        grid_spec=grid_spec,
        out_shape=jax.ShapeDtypeStruct((n, d), F32),
        compiler_params=_params(1),
        name="combine",
    )(pos, x2, route, g_final, y)


def _dispatch(expert_id):
    n_tok = expert_id.shape[0]
    n_slot = n_tok * TOP_K
    flat_e = expert_id.reshape(-1)
    iota_slot = jnp.arange(n_slot, dtype=jnp.int32)
    s_e, order = lax.sort((flat_e, iota_slot), num_keys=1)
    experts = jnp.arange(N_EXPERTS, dtype=jnp.int32)
    counts = jnp.sum((flat_e[:, None] == experts[None, :]).astype(jnp.int32), axis=0)
    start = jnp.cumsum(counts) - counts
    padded = ((counts + MOE_BLOCK - 1) // MOE_BLOCK) * MOE_BLOCK
    pad_end = jnp.cumsum(padded)
    pad_start = pad_end - padded
    n_blk = n_slot // MOE_BLOCK + N_EXPERTS + GATHER_SLOTS - 2
    n_active = (pad_end[-1] // MOE_BLOCK).astype(jnp.int32)
    blk_ids = jnp.arange(n_blk, dtype=jnp.int32)
    blk_first = blk_ids * MOE_BLOCK
    blk_expert = jnp.minimum(jnp.sum((pad_end[None, :] <= blk_first[:, None]).astype(jnp.int32), axis=1),
                             N_EXPERTS - 1)
    blk_expert = jnp.where(blk_ids < n_active, blk_expert, blk_expert[n_active - 1])
    shift = pad_start - start
    onehot = (blk_expert[:, None] == experts[None, :]).astype(jnp.int32)
    blk_src = jnp.clip(blk_first - jnp.sum(onehot * shift[None, :], axis=1), 0, n_slot)
    is_first = (blk_ids < n_active) & (blk_first == jnp.sum(onehot * pad_start[None, :], axis=1))
    nxt_blk = blk_ids + jnp.sum(onehot * (padded // MOE_BLOCK)[None, :], axis=1)
    blk_next = jnp.where(is_first & (nxt_blk < n_active),
                         blk_expert[jnp.minimum(nxt_blk, n_blk - 1)], -1).astype(jnp.int32)
    order_pad = jnp.concatenate([order, jnp.zeros((MOE_BLOCK,), jnp.int32)])
    shift_sorted = jnp.sum((s_e[:, None] == experts[None, :]).astype(jnp.int32) * shift[None, :], axis=1)
    _, pos = lax.sort((order, iota_slot + shift_sorted), num_keys=1)
    return blk_expert, n_active.reshape(1), blk_src, blk_next, order_pad, pos


def _rotate_half_cols(w):
    half = w.shape[-1] // 2
    return jnp.concatenate([-w[..., half:], w[..., :half]], axis=-1)


def _layer(x, mem, positions, g_mix, w_in, w_conv, g_q_a, w_q_up, g_kv_a, w_kv_up, g_conv_out,
           g_attn_out, w_out, g_xattn, g_mem, w_xq, w_xk, w_xv, w_xo, g_moe, w_group, b_group,
           w_router, b_router, w_gate, w_up, w_down):
    batch, seq, d = x.shape
    mem_len = mem.shape[1]
    n = batch * seq
    cw = w_conv.shape[1]
    ql = g_q_a.shape[0]
    kvl = g_kv_a.shape[0]
    c_rope = 3 * cw + ql + kvl
    row2 = lambda g: g.reshape(1, -1)

    w_kr = w_in[:, c_rope:c_rope + QK_ROPE]
    w_in_ext = jnp.concatenate([w_in[:, :c_rope + QK_ROPE], _rotate_half_cols(w_kr)], axis=1).astype(BF16)
    inv_freq = 1.0 / (ROPE_THETA ** (jnp.arange(0, QK_ROPE, 2, dtype=F32) / QK_ROPE))
    freq = jnp.tile(inv_freq, LANES // inv_freq.shape[0]).reshape(1, LANES)

    ycn, cqn, ckvn, kr, cs = _in_proj(
        x.reshape(n, d), positions.reshape(n, 1), row2(g_mix), w_in_ext, w_conv, row2(g_q_a),
        row2(g_kv_a), row2(g_conv_out), freq, seq=seq, cw=cw, ql=ql, kvl=kvl)

    wq = w_q_up.reshape(ql, N_MLA_HEADS, QK_NOPE + QK_ROPE).transpose(1, 0, 2)
    wq_nope, wq_rope = wq[..., :QK_NOPE], wq[..., QK_NOPE:]
    z64 = jnp.zeros((N_MLA_HEADS, ql, QK_ROPE), F32)
    wqa = jnp.concatenate([wq_nope, wq_rope, z64], axis=-1).astype(BF16)
    wqb = jnp.concatenate([jnp.zeros_like(wq_nope), _rotate_half_cols(wq_rope), z64], axis=-1).astype(BF16)
    wkv = w_kv_up.reshape(kvl, N_MLA_HEADS, QK_NOPE + V_HEAD).transpose(1, 0, 2).astype(BF16)
    wk = wkv[..., :QK_NOPE]
    wvt = wkv[..., QK_NOPE:].transpose(0, 2, 1)

    yan = _attention(cqn, cs, ckvn, kr, wqa, wqb, wk, wvt, row2(g_attn_out), batch=batch, seq=seq)

    w_xkv = jnp.concatenate([w_xk, w_xv], axis=1).astype(BF16)
    kmem, vmem = _mem_kv(mem.reshape(batch * mem_len, d), row2(g_mem), w_xkv, mem_len=mem_len)

    pad = LANES - N_GROUPS - N_EXPERTS
    w_rt = jnp.concatenate([w_group, w_router, jnp.zeros((d, pad), F32)], axis=1)
    w_rt_hi = w_rt.astype(BF16)
    w_rt_lo = (w_rt - w_rt_hi.astype(F32)).astype(BF16)
    b_rt = jnp.concatenate([b_group, b_router, jnp.zeros((pad,), F32)]).reshape(1, LANES)
    x2, xp, route = _mid(x.reshape(n, d), ycn, yan, w_out.astype(BF16), row2(g_xattn), w_xq.astype(BF16),
                         kmem, vmem, w_xo.astype(BF16), row2(g_moe), w_rt_hi, w_rt_lo, b_rt,
                         seq=seq, mem_len=mem_len)

    expert_id = route[:, :TOP_K].astype(jnp.int32)
    blk_expert, n_active, blk_src, blk_next, order_pad, pos = _dispatch(expert_id)
    y = _experts(blk_expert, n_active, blk_src, blk_next, order_pad, xp, w_gate, w_up, w_down)
    pos_tiles = pos.reshape(n // ROW_TILE, ROW_TILE, TOP_K).transpose(0, 2, 1).reshape(-1)
    return x2, route, pos_tiles, y


def kernel(x, mem, positions, g_mix, w_in, w_conv, g_q_a, w_q_up, g_kv_a, w_kv_up, g_conv_out, g_attn_out, w_out, g_xattn, g_mem, w_xq, w_xk, w_xv, w_xo, g_moe, w_group, b_group, w_router, b_router, w_gate, w_up, w_down, g_final):
    depth = g_mix.shape[0]
    assert depth == 1, "the combine stage applies the final norm, so exactly one layer is supported"
    shape = x.shape
    x2, route, pos_tiles, y = _layer(
        x, mem, positions, g_mix[0], w_in[0], w_conv[0], g_q_a[0], w_q_up[0], g_kv_a[0], w_kv_up[0],
        g_conv_out[0], g_attn_out[0], w_out[0], g_xattn[0], g_mem[0], w_xq[0], w_xk[0], w_xv[0],
        w_xo[0], g_moe[0], w_group[0], b_group[0], w_router[0], b_router[0], w_gate[0], w_up[0], w_down[0])
    out = _combine(pos_tiles, x2, route, g_final.reshape(1, -1), y)
    return out.reshape(shape)
```
